```python
import jax
import jax.numpy as jnp
from jax import lax
import numpy as np

D_MODEL = 1024
BATCH = 32
SEQ = 2048
DEPTH = 2
DEC_BATCH = 16
DEC_SEQ = 16
PAST_LEN = 2048

CHUNK = 64
N_A_LAYERS = DEPTH // 2
N_B_LAYERS = DEPTH - N_A_LAYERS
EPS = 1e-6
GLA_HEADS = 4
GLA_DK = D_MODEL // 2 // GLA_HEADS
GLA_DV = D_MODEL // GLA_HEADS
GLA_GATE_RANK = 16
GLA_GATE_TAU = 16.0
GLA_QK = GLA_HEADS * GLA_DK
GLA_V = GLA_HEADS * GLA_DV
GLA_SPLITS = (GLA_QK, 2 * GLA_QK, 2 * GLA_QK + GLA_V, 2 * GLA_QK + GLA_V + GLA_GATE_RANK)
GLA_IN_COLS = 2 * GLA_QK + 2 * GLA_V + GLA_GATE_RANK
FOX_HEADS = 16
FOX_HEAD_DIM = D_MODEL // FOX_HEADS
FOX_WIDTH = FOX_HEADS * FOX_HEAD_DIM
Q_BLOCK = 128
N_GROUPS = 4
EXPERTS_PER_GROUP = 8
N_EXPERTS = N_GROUPS * EXPERTS_PER_GROUP
TOP_K = 2
D_EXPERT = D_MODEL // 2

kernel_name = 'yoco_gla_fox_hmoe_stream_step'


def rms_norm(x, g):
    xf = x.astype(jnp.float32)
    y = xf * lax.rsqrt(jnp.mean(xf * xf, axis=-1, keepdims=True) + EPS)
    return (y * g.astype(jnp.float32)).astype(x.dtype)


def gla_chunk(S, q, k, v, la):
    L = q.shape[1]
    b = jnp.cumsum(la, axis=1)
    b_ref = b[:, L // 2][:, None]
    qf, kf, vf = q.astype(jnp.float32), k.astype(jnp.float32), v.astype(jnp.float32)
    att = jnp.einsum('bthk,bshk->bhts', qf * jnp.exp(b - b_ref), kf * jnp.exp(b_ref - b))
    att = jnp.where(jnp.tril(jnp.ones((L, L), bool)), att, 0.0)
    o = jnp.einsum('bhts,bshv->bthv', att, vf) + jnp.einsum('bthk,bhkv->bthv', qf * jnp.exp(b), S)
    b_last = b[:, -1]
    S_new = S * jnp.exp(b_last)[..., None] + jnp.einsum('bshk,bshv->bhkv', kf * jnp.exp(b_last[:, None] - b), vf)
    return S_new, o


def gla_mixer(hn, S0, w_in, w_gate2, b_gate, g_head, w_out):
    B, L, _ = hn.shape
    q, k, v, g_lr, r = jnp.split(hn @ w_in, GLA_SPLITS, axis=-1)
    la = jax.nn.log_sigmoid((g_lr @ w_gate2 + b_gate).astype(jnp.float32)) / GLA_GATE_TAU
    q = q.reshape(B, L, GLA_HEADS, GLA_DK) * (GLA_DK ** -0.5)
    k = k.reshape(B, L, GLA_HEADS, GLA_DK)
    v = v.reshape(B, L, GLA_HEADS, GLA_DV)
    la = la.reshape(B, L, GLA_HEADS, GLA_DK)
    S0 = S0.astype(jnp.float32)
    if L <= CHUNK:
        S, o = gla_chunk(S0, q, k, v, la)
    else:
        n = L // CHUNK
        to_chunks = lambda a: jnp.moveaxis(a.reshape(B, n, CHUNK, *a.shape[2:]), 1, 0)
        S, o = lax.scan(lambda s, xs: gla_chunk(s, *xs), S0, (to_chunks(q), to_chunks(k), to_chunks(v), to_chunks(la)))
        o = jnp.moveaxis(o, 0, 1).reshape(B, L, GLA_HEADS, GLA_DV)
    o = rms_norm(o, g_head).astype(hn.dtype).reshape(B, L, GLA_V) * jax.nn.silu(r)
    return o @ w_out, S


def shared_kv(h, g_kv, w_kv, b_forget, g_k):
    B, L, _ = h.shape
    k, v, fl = jnp.split(rms_norm(h, g_kv) @ w_kv, (FOX_WIDTH, 2 * FOX_WIDTH), axis=-1)
    k = rms_norm(k.reshape(B, L, FOX_HEADS, FOX_HEAD_DIM), g_k)
    v = v.reshape(B, L, FOX_HEADS, FOX_HEAD_DIM)
    logf = jax.nn.log_sigmoid((fl + b_forget).astype(jnp.float32))
    return k, v, logf


def fox_attention(hn, k_all, v_all, c_all, n_past, w_qg, g_q, w_o):
    B, L, _ = hn.shape
    q, gate = jnp.split(hn @ w_qg, 2, axis=-1)
    q = rms_norm(q.reshape(B, L, FOX_HEADS, FOX_HEAD_DIM), g_q) * (FOX_HEAD_DIM ** -0.5)
    c_q = c_all[:, n_past:n_past + L]
    q_pos = n_past + jnp.arange(L)
    k_pos = jnp.arange(k_all.shape[1])
    c_k = jnp.swapaxes(c_all, 1, 2)[:, :, None, :]

    def block(args):
        qb, cb, pb = args
        s = jnp.einsum('bqhd,bkhd->bhqk', qb, k_all, preferred_element_type=jnp.float32)
        s = s + jnp.swapaxes(cb, 1, 2)[..., None] - c_k
        s = jnp.where(k_pos[None, :] <= pb[:, None], s, -jnp.inf)
        p = jax.nn.softmax(s, axis=-1)
        return jnp.einsum('bhqk,bkhd->bqhd', p.astype(v_all.dtype), v_all)

    if L <= Q_BLOCK:
        o = block((q, c_q, q_pos))
    else:
        nb = L // Q_BLOCK
        split = lambda a: jnp.moveaxis(a.reshape(B, nb, Q_BLOCK, *a.shape[2:]), 1, 0)
        o = lax.map(block, (split(q), split(c_q), q_pos.reshape(nb, Q_BLOCK)))
        o = jnp.moveaxis(o, 0, 1)
    o = o.reshape(B, L, FOX_WIDTH).astype(hn.dtype) * jax.nn.sigmoid(gate)
    return o @ w_o


def hier_moe(hn, w_group, b_group, w_router, b_router, w_gate, w_up, w_down):
    B, L, D = hn.shape
    x = hn.reshape(B * L, D)
    g_prob = jax.nn.softmax((x @ w_group).astype(jnp.float32) + b_group, axis=-1)
    g_top, g_idx = lax.top_k(g_prob, 1)
    e_logits = (x @ w_router).astype(jnp.float32).reshape(-1, N_GROUPS, EXPERTS_PER_GROUP) + b_router
    e_logits = jnp.einsum('tge,tg->te', e_logits, jax.nn.one_hot(g_idx[:, 0], N_GROUPS, dtype=jnp.float32))
    e_top, e_idx = lax.top_k(jax.nn.softmax(e_logits, axis=-1), TOP_K)
    weights = g_top * e_top / jnp.sum(e_top, axis=-1, keepdims=True)
    expert_id = g_idx * EXPERTS_PER_GROUP + e_idx
    gates = jnp.einsum('tk,tke->te', weights, jax.nn.one_hot(expert_id, N_EXPERTS, dtype=jnp.float32)).astype(x.dtype)
    y = jnp.zeros_like(x)
    for e in range(N_EXPERTS):
        hid = jax.nn.silu(x @ w_gate[e]) * (x @ w_up[e])
        y = y + (gates[:, e:e + 1] * hid) @ w_down[e]
    return y.reshape(B, L, D)


def trunk(x, S_in, past_k, past_v, past_logf, p):
    h = x
    S_out = []
    k_new = v_new = logf_new = None
    k_all = v_all = c_all = None
    n_past = 0 if past_k is None else past_k.shape[1]
    for layer in range(DEPTH):
        hn = rms_norm(h, p['norm_mix'][layer])
        if layer < N_A_LAYERS:
            y, S = gla_mixer(hn, S_in[layer], p['w_gla_in'][layer], p['w_gla_gate2'][layer],
                             p['b_gla_gate'][layer], p['g_gla_head'][layer], p['w_gla_out'][layer])
            S_out.append(S)
        else:
            if layer == N_A_LAYERS:
                k_new, v_new, logf_new = shared_kv(h, p['g_kv'], p['w_kv'], p['b_forget'], p['g_k'])
                if past_k is None:
                    k_all, v_all, logf_all = k_new, v_new, logf_new
                else:
                    k_all = jnp.concatenate([past_k.astype(k_new.dtype), k_new], axis=1)
                    v_all = jnp.concatenate([past_v.astype(v_new.dtype), v_new], axis=1)
                    logf_all = jnp.concatenate([past_logf.astype(jnp.float32), logf_new], axis=1)
                c_all = jnp.cumsum(logf_all, axis=1)
            bl = layer - N_A_LAYERS
            y = fox_attention(hn, k_all, v_all, c_all, n_past, p['w_fox_qg'][bl], p['g_q'][bl], p['w_fox_out'][bl])
        h = h + y
        h = h + hier_moe(rms_norm(h, p['norm_ffn'][layer]), p['w_group'][layer], p['b_group'][layer],
                         p['w_router'][layer], p['b_router'][layer], p['w_exp_gate'][layer],
                         p['w_exp_up'][layer], p['w_exp_down'][layer])
    return h, k_new, v_new, logf_new, jnp.stack(S_out)


def setup_inputs(seed: int = 0) -> dict:
    key = jax.random.key(seed)
    ks = iter(jax.random.split(key, 40))
    nrm = lambda shape, scale: jax.random.normal(next(ks), shape, jnp.float32) * scale
    D = D_MODEL
    return {
        'x_prompt': nrm((BATCH, SEQ, D), 1.0),
        'x_sample': nrm((DEC_BATCH, DEC_SEQ, D), 1.0),
        'cache_k': nrm((DEC_BATCH, PAST_LEN, FOX_HEADS, FOX_HEAD_DIM), 1.0),
        'cache_v': nrm((DEC_BATCH, PAST_LEN, FOX_HEADS, FOX_HEAD_DIM), 1.0),
        'cache_logf': jax.nn.log_sigmoid(2.0 + nrm((DEC_BATCH, PAST_LEN, FOX_HEADS), 1.0)),
        'state_gla': nrm((N_A_LAYERS, DEC_BATCH, GLA_HEADS, GLA_DK, GLA_DV), 0.5),
        'norm_mix': 1.0 + nrm((DEPTH, D), 0.02),
        'norm_ffn': 1.0 + nrm((DEPTH, D), 0.02),
        'w_gla_in': nrm((N_A_LAYERS, D, GLA_IN_COLS), D ** -0.5),
        'w_gla_gate2': nrm((N_A_LAYERS, GLA_GATE_RANK, GLA_QK), GLA_GATE_RANK ** -0.5),
        'b_gla_gate': nrm((N_A_LAYERS, GLA_QK), 0.1),
        'g_gla_head': 1.0 + nrm((N_A_LAYERS, GLA_DV), 0.02),
        'w_gla_out': nrm((N_A_LAYERS, GLA_V, D), GLA_V ** -0.5),
        'g_kv': 1.0 + nrm((D,), 0.02),
        'w_kv': nrm((D, 2 * FOX_WIDTH + FOX_HEADS), D ** -0.5),
        'b_forget': 2.0 + nrm((FOX_HEADS,), 0.5),
        'g_k': 1.0 + nrm((FOX_HEAD_DIM,), 0.02),
        'w_fox_qg': nrm((N_B_LAYERS, D, 2 * FOX_WIDTH), D ** -0.5),
        'g_q': 1.0 + nrm((N_B_LAYERS, FOX_HEAD_DIM), 0.02),
        'w_fox_out': nrm((N_B_LAYERS, FOX_WIDTH, D), FOX_WIDTH ** -0.5),
        'w_group': nrm((DEPTH, D, N_GROUPS), D ** -0.5),
        'b_group': nrm((DEPTH, N_GROUPS), 0.01),
        'w_router': nrm((DEPTH, D, N_EXPERTS), D ** -0.5),
        'b_router': nrm((DEPTH, N_GROUPS, EXPERTS_PER_GROUP), 0.01),
        'w_exp_gate': nrm((DEPTH, N_EXPERTS, D, D_EXPERT), D ** -0.5),
        'w_exp_up': nrm((DEPTH, N_EXPERTS, D, D_EXPERT), D ** -0.5),
        'w_exp_down': nrm((DEPTH, N_EXPERTS, D_EXPERT, D), D_EXPERT ** -0.5),
    }


def reference(x_prompt, x_sample, cache_k, cache_v, cache_logf, state_gla, norm_mix, norm_ffn,
              w_gla_in, w_gla_gate2, b_gla_gate, g_gla_head, w_gla_out, g_kv, w_kv, b_forget, g_k,
              w_fox_qg, g_q, w_fox_out, w_group, b_group, w_router, b_router, w_exp_gate, w_exp_up, w_exp_down):
    params = dict(norm_mix=norm_mix, norm_ffn=norm_ffn, w_gla_in=w_gla_in, w_gla_gate2=w_gla_gate2,
                  b_gla_gate=b_gla_gate, g_gla_head=g_gla_head, w_gla_out=w_gla_out, g_kv=g_kv, w_kv=w_kv,
                  b_forget=b_forget, g_k=g_k, w_fox_qg=w_fox_qg, g_q=g_q, w_fox_out=w_fox_out,
                  w_group=w_group, b_group=b_group, w_router=w_router, b_router=b_router,
                  w_exp_gate=w_exp_gate, w_exp_up=w_exp_up, w_exp_down=w_exp_down)
    s0 = jnp.zeros((N_A_LAYERS, x_prompt.shape[0], GLA_HEADS, GLA_DK, GLA_DV), jnp.float32)
    y_prompt, k_prompt, v_prompt, logf_prompt, gla_prompt = trunk(x_prompt, s0, None, None, None, params)
    y_sample, k_sample, v_sample, logf_sample, gla_sample = trunk(x_sample, state_gla, cache_k, cache_v, cache_logf, params)
    return (y_prompt, y_sample, k_prompt, v_prompt, logf_prompt, gla_prompt, k_sample, v_sample, logf_sample, gla_sample)
```

```python
import functools

import jax
import jax.numpy as jnp
from jax import lax
from jax.experimental import pallas as pl
from jax.experimental.pallas import tpu as pltpu

F32 = jnp.float32
BF16 = jnp.bfloat16
U32 = jnp.uint32
I32 = jnp.int32

EPS = 1e-6
D_MODEL = 1024
HALF_D = D_MODEL // 2
LANES = 128
VMEM_LIMIT_BYTES = 56 * 1024 * 1024

GLA_HEADS = 4
GLA_DK = 128
GLA_DV = 256
GLA_QK = GLA_HEADS * GLA_DK
GLA_V = GLA_HEADS * GLA_DV
GLA_GATE_RANK = 16
GLA_GATE_TAU = 16.0
GLA_CHUNK = 64
GLA_GROUP = 256

FOX_HEADS = 16
FOX_DH = 64
FOX_PAIRS = FOX_HEADS // 2

N_GROUPS = 4
EXPERTS_PER_GROUP = 8
N_EXPERTS = N_GROUPS * EXPERTS_PER_GROUP
D_EXPERT = 512
MOE_BLOCK = 256


def _cparams(sem):
    return pltpu.CompilerParams(dimension_semantics=sem, vmem_limit_bytes=VMEM_LIMIT_BYTES)


def _row_tile(t, pref):
    tm = min(t, pref)
    assert t % tm == 0
    return tm


def _rms_scale(x):
    return lax.rsqrt(jnp.mean(x * x, axis=-1, keepdims=True) + EPS)


def _pack_halves(y):
    lo = lax.bitcast_convert_type(y[:, :HALF_D].astype(BF16).astype(F32), U32)
    hi = lax.bitcast_convert_type(y[:, HALF_D:].astype(BF16).astype(F32), U32)
    return (lo >> 16) | (hi & jnp.uint32(0xFFFF0000))


def _unpack_halves(p):
    lo = lax.bitcast_convert_type(p << 16, F32)
    hi = lax.bitcast_convert_type(p & jnp.uint32(0xFFFF0000), F32)
    return lo, hi


def _gla_in_kernel(h_ref, g_ref, wqk_ref, wv_ref, wr_ref, wg_ref, wg2_ref, bg_ref,
                   q_ref, k_ref, v_ref, r_ref, la_ref):
    x = h_ref[...]
    hn = (x * _rms_scale(x) * g_ref[...]).astype(BF16)
    qk = jnp.dot(hn, wqk_ref[...], preferred_element_type=F32)
    q_ref[...] = (qk[:, :GLA_QK] * (GLA_DK ** -0.5)).astype(BF16)
    k_ref[...] = qk[:, GLA_QK:].astype(BF16)
    v_ref[...] = jnp.dot(hn, wv_ref[...], preferred_element_type=F32).astype(BF16)
    r_ref[...] = jnp.dot(hn, wr_ref[...], preferred_element_type=F32).astype(BF16)
    g_lr = jnp.dot(hn, wg_ref[...], preferred_element_type=F32)
    z = jnp.dot(g_lr.astype(BF16), wg2_ref[...], preferred_element_type=F32) + bg_ref[...]
    la_ref[...] = jax.nn.log_sigmoid(z) / GLA_GATE_TAU


def _gla_in_proj(h, g, wqk, wv, wr, wg, wg2, bg):
    t = h.shape[0]
    tm = _row_tile(t, 512)
    row = lambda n: pl.BlockSpec((tm, n), lambda i: (i, 0))
    full = lambda a: pl.BlockSpec(a.shape, lambda i: (0,) * a.ndim)
    return pl.pallas_call(
        _gla_in_kernel,
        grid=(t // tm,),
        in_specs=[row(D_MODEL), full(g), full(wqk), full(wv), full(wr), full(wg), full(wg2), full(bg)],
        out_specs=[row(GLA_QK), row(GLA_QK), row(GLA_V), row(GLA_V), row(GLA_QK)],
        out_shape=[jax.ShapeDtypeStruct((t, GLA_QK), BF16), jax.ShapeDtypeStruct((t, GLA_QK), BF16),
                   jax.ShapeDtypeStruct((t, GLA_V), BF16), jax.ShapeDtypeStruct((t, GLA_V), BF16),
                   jax.ShapeDtypeStruct((t, GLA_QK), F32)],
        compiler_params=_cparams(("parallel",)),
        name="gla_in_proj",
    )(h, g, wqk, wv, wr, wg, wg2, bg)


def _gla_kernel(q_ref, k_ref, v_ref, la_ref, s0_ref, gh_ref, o_ref, s_ref,
                qb_s, oi_s, u_s, dl_s, *, seq, chunk, group):
    n_groups = seq // group
    n_chunks = seq // chunk
    per_group = group // chunk

    shift = chunk.bit_length() - 1
    assert chunk == 1 << shift and group & (group - 1) == 0
    row = lax.broadcasted_iota(I32, (group, group), 0)
    col = lax.broadcasted_iota(I32, (group, group), 1)
    tril = ((row >> shift) == (col >> shift)) & (col <= row)
    row3 = lax.broadcasted_iota(I32, (3 * group, group), 0)
    col3 = lax.broadcasted_iota(I32, (3 * group, group), 1)
    r3 = row3 & (group - 1)
    kind = row3 >> (group.bit_length() - 1)
    upto = jnp.where(kind == 0, r3 & (chunk - 1), jnp.where(kind == 1, chunk // 2, chunk - 1))
    sel = ((r3 >> shift) == (col3 >> shift)) & ((col3 & (chunk - 1)) <= upto)
    sel = jnp.where(sel, 1.0, 0.0).astype(BF16)

    def group_body(g, carry):
        r0 = pl.multiple_of(g * group, group)
        rows = pl.ds(r0, group)
        la = la_ref[rows, :]
        la_hi = la.astype(BF16)
        la_lo = (la - la_hi.astype(F32)).astype(BF16)
        cs = (jnp.dot(sel, la_hi, preferred_element_type=F32)
              + jnp.dot(sel, la_lo, preferred_element_type=F32))
        b = cs[:group]
        b_ref = cs[group:2 * group]
        b_last = cs[2 * group:]
        qf = q_ref[rows, :].astype(F32)
        kf = k_ref[rows, :].astype(F32)
        vg = v_ref[rows, :]
        qe = (qf * jnp.exp(b - b_ref)).astype(BF16)
        ke = (kf * jnp.exp(b_ref - b)).astype(BF16)
        att = lax.dot_general(qe, ke, (((1,), (1,)), ((), ())), preferred_element_type=F32)
        att = jnp.where(tril, att, 0.0).astype(BF16)
        oi_s[rows, :] = jnp.dot(att, vg, preferred_element_type=F32)
        qb_s[rows, :] = (qf * jnp.exp(b)).astype(BF16)
        kl = (kf * jnp.exp(b_last - b)).astype(BF16)
        decay = jnp.exp(b_last)
        for j in range(per_group):
            c = g * per_group + j
            sl = slice(j * chunk, (j + 1) * chunk)
            u_s[c] = lax.dot_general(vg[sl], kl[sl], (((0,), (0,)), ((), ())), preferred_element_type=F32)
            dl_s[c] = decay[j * chunk:j * chunk + 1]
        return carry

    lax.fori_loop(0, n_groups, group_body, 0)

    gh = gh_ref[...]

    def chunk_body(c, st):
        r0 = pl.multiple_of(c * chunk, chunk)
        rows = pl.ds(r0, chunk)
        o = oi_s[rows, :] + lax.dot_general(qb_s[rows, :], st.astype(BF16), (((1,), (1,)), ((), ())),
                                            preferred_element_type=F32)
        o_ref[rows, :] = (o * _rms_scale(o) * gh).astype(BF16)
        return st * dl_s[c] + u_s[c]

    st = lax.fori_loop(0, n_chunks, chunk_body, s0_ref[0, 0].T)
    s_ref[0, 0] = st.T


def _gla(q, k, v, la, s0, g_head, batch, seq):
    chunk = min(seq, GLA_CHUNK)
    group = min(seq, GLA_GROUP)
    assert seq % group == 0 and group % chunk == 0
    n_chunks = seq // chunk
    kern = functools.partial(_gla_kernel, seq=seq, chunk=chunk, group=group)
    return pl.pallas_call(
        kern,
        grid=(batch, GLA_HEADS),
        in_specs=[pl.BlockSpec((seq, GLA_DK), lambda b, h: (b, h)),
                  pl.BlockSpec((seq, GLA_DK), lambda b, h: (b, h)),
                  pl.BlockSpec((seq, GLA_DV), lambda b, h: (b, h)),
                  pl.BlockSpec((seq, GLA_DK), lambda b, h: (b, h)),
                  pl.BlockSpec((1, 1, GLA_DK, GLA_DV), lambda b, h: (b, h, 0, 0)),
                  pl.BlockSpec((1, GLA_DV), lambda b, h: (0, 0))],
        out_specs=[pl.BlockSpec((seq, GLA_DV), lambda b, h: (b, h)),
                   pl.BlockSpec((1, 1, GLA_DK, GLA_DV), lambda b, h: (b, h, 0, 0))],
        out_shape=[jax.ShapeDtypeStruct((batch * seq, GLA_V), BF16),
                   jax.ShapeDtypeStruct((batch, GLA_HEADS, GLA_DK, GLA_DV), F32)],
        scratch_shapes=[pltpu.VMEM((seq, GLA_DK), BF16),
                        pltpu.VMEM((seq, GLA_DV), F32),
                        pltpu.VMEM((n_chunks, GLA_DV, GLA_DK), F32),
                        pltpu.VMEM((n_chunks, 1, GLA_DK), F32)],
        compiler_params=_cparams(("parallel", "parallel")),
        name="gla_scan",
    )(q, k, v, la, s0, g_head)


def _out_proj_kernel(*refs, gated):
    if gated:
        a_ref, r_ref, w_ref, h_ref, o_ref = refs
        r = r_ref[...].astype(F32)
        a = (a_ref[...].astype(F32) * (r * jax.nn.sigmoid(r))).astype(BF16)
    else:
        a_ref, w_ref, h_ref, o_ref = refs
        a = a_ref[...]
    o_ref[...] = h_ref[...] + jnp.dot(a, w_ref[...], preferred_element_type=F32)


def _out_proj(a, r, w, h):
    t = h.shape[0]
    tm = _row_tile(t, 512)
    row = pl.BlockSpec((tm, D_MODEL), lambda i: (i, 0))
    wspec = pl.BlockSpec(w.shape, lambda i: (0, 0))
    gated = r is not None
    args = (a, r, w, h) if gated else (a, w, h)
    specs = [row, row, wspec, row] if gated else [row, wspec, row]
    return pl.pallas_call(
        functools.partial(_out_proj_kernel, gated=gated),
        grid=(t // tm,),
        in_specs=specs,
        out_specs=row,
        out_shape=jax.ShapeDtypeStruct((t, D_MODEL), F32),
        compiler_params=_cparams(("parallel",)),
        name="out_proj_gated" if gated else "out_proj",
    )(*args)


REC_E0, REC_E1, REC_W0, REC_W1, REC_R0, REC_R1 = range(6)
REC_WIDTH = 8


def _router_kernel(h_ref, g_ref, whi_ref, wlo_ref, bias_ref, xp_ref, rec_ref, cnt_ref, carry_s, *, tm):
    @pl.when(pl.program_id(0) == 0)
    def _():
        carry_s[...] = jnp.zeros_like(carry_s)

    x = h_ref[...]
    hn = x * _rms_scale(x) * g_ref[...]
    xp_ref[...] = _pack_halves(hn)

    hi = hn.astype(BF16)
    lo = (hn - hi.astype(F32)).astype(BF16)
    w_hi = whi_ref[...]
    lg = (jnp.dot(hi, w_hi, preferred_element_type=F32) + jnp.dot(lo, w_hi, preferred_element_type=F32)
          + jnp.dot(hi, wlo_ref[...], preferred_element_type=F32) + bias_ref[...])

    lane = lax.broadcasted_iota(I32, (tm, LANES), 1).astype(F32)
    neg = jnp.float32(-jnp.inf)

    def masked_softmax(mask):
        m = jnp.max(jnp.where(mask, lg, neg), axis=1, keepdims=True)
        e = jnp.where(mask, jnp.exp(lg - m), 0.0)
        return e / jnp.sum(e, axis=1, keepdims=True)

    def top1(p, mask):
        v = jnp.max(jnp.where(mask, p, -1.0), axis=1, keepdims=True)
        idx = jnp.min(jnp.where(mask & (p == v), lane, float(LANES)), axis=1, keepdims=True)
        return v, idx

    gmask = lane < N_GROUPS
    g_top, g_idx = top1(masked_softmax(gmask), gmask)
    e_lo = N_GROUPS + EXPERTS_PER_GROUP * g_idx
    emask = (lane >= e_lo) & (lane < e_lo + EXPERTS_PER_GROUP)
    ep = masked_softmax(emask)
    p1, i1 = top1(ep, emask)
    mask2 = emask & (lane != i1)
    p2, i2 = top1(ep, mask2)
    e0 = i1 - N_GROUPS
    e1 = i2 - N_GROUPS
    denom = p1 + p2
    w0 = g_top * p1 / denom
    w1 = g_top * p2 / denom

    hit0 = lane == e0
    hit1 = lane == e1
    onehot = jnp.where(hit0 | hit1, 1.0, 0.0)
    r_i = lax.broadcasted_iota(I32, (tm, tm), 0)
    c_i = lax.broadcasted_iota(I32, (tm, tm), 1)
    before = jnp.where(c_i < r_i, 1.0, 0.0).astype(BF16)
    prior = jnp.dot(before, onehot.astype(BF16), preferred_element_type=F32) + carry_s[...]
    rank0 = jnp.sum(jnp.where(hit0, prior, 0.0), axis=1, keepdims=True)
    rank1 = jnp.sum(jnp.where(hit1, prior, 0.0), axis=1, keepdims=True)
    carry_s[...] = carry_s[...] + jnp.sum(onehot, axis=0, keepdims=True)
    cnt_ref[...] = carry_s[...]

    rec = jnp.zeros((tm, LANES), F32)
    for slot, val in ((REC_E0, e0), (REC_E1, e1), (REC_W0, w0), (REC_W1, w1), (REC_R0, rank0), (REC_R1, rank1)):
        rec = jnp.where(lane == slot, val, rec)
    rec_ref[...] = rec[:, :REC_WIDTH]


def _router(h, g, w_hi, w_lo, bias):
    t = h.shape[0]
    tm = _row_tile(t, 512)
    full = lambda a: pl.BlockSpec(a.shape, lambda i: (0,) * a.ndim)
    return pl.pallas_call(
        functools.partial(_router_kernel, tm=tm),
        grid=(t // tm,),
        in_specs=[pl.BlockSpec((tm, D_MODEL), lambda i: (i, 0)), full(g), full(w_hi), full(w_lo), full(bias)],
        out_specs=[pl.BlockSpec((tm, HALF_D), lambda i: (i, 0)),
                   pl.BlockSpec((tm, REC_WIDTH), lambda i: (i, 0)),
                   pl.BlockSpec((1, LANES), lambda i: (0, 0))],
        out_shape=[jax.ShapeDtypeStruct((t, HALF_D), U32),
                   jax.ShapeDtypeStruct((t, REC_WIDTH), F32),
                   jax.ShapeDtypeStruct((1, LANES), F32)],
        scratch_shapes=[pltpu.VMEM((1, LANES), F32)],
        compiler_params=_cparams(("arbitrary",)),
        name="moe_router",
    )(h, g, w_hi, w_lo, bias)


def _row_copy(src, src_row, dst, dst_row, sem):
    return pltpu.make_async_copy(src.at[pl.ds(src_row, 1)], dst.at[pl.ds(dst_row, 1)], sem)


def _dispatch_kernel(pos_ref, xp_ref, zeros_hbm, xs_hbm, sem, *, tm):
    del zeros_hbm

    def issue(r, carry):
        _row_copy(xp_ref, r, xs_hbm, pos_ref[0, 0, r], sem.at[0]).start()
        _row_copy(xp_ref, r, xs_hbm, pos_ref[0, 0, tm + r], sem.at[0]).start()
        return carry

    lax.fori_loop(0, tm, issue, 0)

    def drain(r, carry):
        _row_copy(xp_ref, 0, xs_hbm, 0, sem.at[0]).wait()
        _row_copy(xp_ref, 0, xs_hbm, 0, sem.at[0]).wait()
        return carry

    lax.fori_loop(0, tm, drain, 0)


def _dispatch(pos, xp, n_slots):
    t = xp.shape[0]
    tm = pos.shape[2] // 2
    return pl.pallas_call(
        functools.partial(_dispatch_kernel, tm=tm),
        grid=(t // tm,),
        in_specs=[pl.BlockSpec((1, 1, 2 * tm), lambda i: (i, 0, 0), memory_space=pltpu.SMEM),
                  pl.BlockSpec((tm, HALF_D), lambda i: (i, 0)),
                  pl.BlockSpec(memory_space=pl.ANY)],
        out_specs=pl.BlockSpec(memory_space=pl.ANY),
        out_shape=jax.ShapeDtypeStruct((n_slots, HALF_D), U32),
        input_output_aliases={2: 0},
        scratch_shapes=[pltpu.SemaphoreType.DMA((1,))],
        compiler_params=_cparams(("arbitrary",)),
        name="moe_dispatch",
    )(pos, xp, jnp.zeros((n_slots, HALF_D), U32))


def _expert_kernel(be_ref, nb_ref, xs_ref, wgu_ref, wd_ref, out_ref):
    @pl.when(pl.program_id(0) < nb_ref[0])
    def _():
        lo, hi = _unpack_halves(xs_ref[...])
        gu = (jnp.dot(lo.astype(BF16), wgu_ref[0, :HALF_D, :], preferred_element_type=F32)
              + jnp.dot(hi.astype(BF16), wgu_ref[0, HALF_D:, :], preferred_element_type=F32))
        gate = gu[:, :D_EXPERT]
        hid = (gate * jax.nn.sigmoid(gate) * gu[:, D_EXPERT:]).astype(BF16)
        out_ref[...] = _pack_halves(jnp.dot(hid, wd_ref[0], preferred_element_type=F32))

    @pl.when(pl.program_id(0) >= nb_ref[0])
    def _():
        out_ref[...] = jnp.zeros_like(out_ref)


def _experts(block_expert, n_blocks_used, xs, w_gu, w_down):
    n_slots = xs.shape[0]
    n_blocks = n_slots // MOE_BLOCK

    def used(b, nb):
        return jnp.minimum(b, nb[0] - 1)

    grid_spec = pltpu.PrefetchScalarGridSpec(
        num_scalar_prefetch=2,
        grid=(n_blocks,),
        in_specs=[pl.BlockSpec((MOE_BLOCK, HALF_D), lambda b, be, nb: (used(b, nb), 0)),
                  pl.BlockSpec((1, D_MODEL, 2 * D_EXPERT), lambda b, be, nb: (be[used(b, nb)], 0, 0)),
                  pl.BlockSpec((1, D_EXPERT, D_MODEL), lambda b, be, nb: (be[used(b, nb)], 0, 0))],
        out_specs=pl.BlockSpec((MOE_BLOCK, HALF_D), lambda b, be, nb: (b, 0)),
    )
    return pl.pallas_call(
        _expert_kernel,
        grid_spec=grid_spec,
        out_shape=jax.ShapeDtypeStruct((n_slots, HALF_D), U32),
        compiler_params=_cparams(("arbitrary",)),
        name="moe_experts",
    )(block_expert, n_blocks_used, xs, w_gu, w_down)


def _combine_kernel(pos_ref, h_ref, rec_ref, out_hbm, y_ref, g0_s, g1_s, sem, *, tm):
    def issue(r, carry):
        _row_copy(out_hbm, pos_ref[0, 0, r], g0_s, r, sem.at[0]).start()
        _row_copy(out_hbm, pos_ref[0, 0, tm + r], g1_s, r, sem.at[0]).start()
        return carry

    lax.fori_loop(0, tm, issue, 0)

    def drain(r, carry):
        _row_copy(out_hbm, 0, g0_s, 0, sem.at[0]).wait()
        _row_copy(out_hbm, 0, g1_s, 0, sem.at[0]).wait()
        return carry

    lax.fori_loop(0, tm, drain, 0)

    rec = rec_ref[...]
    w0 = rec[:, REC_W0:REC_W0 + 1]
    w1 = rec[:, REC_W1:REC_W1 + 1]
    lo0, hi0 = _unpack_halves(g0_s[...])
    lo1, hi1 = _unpack_halves(g1_s[...])
    y_ref[:, :HALF_D] = h_ref[:, :HALF_D] + w0 * lo0 + w1 * lo1
    y_ref[:, HALF_D:] = h_ref[:, HALF_D:] + w0 * hi0 + w1 * hi1


def _combine(pos, h, rec, out_slots):
    t = h.shape[0]
    tm = pos.shape[2] // 2
    return pl.pallas_call(
        functools.partial(_combine_kernel, tm=tm),
        grid=(t // tm,),
        in_specs=[pl.BlockSpec((1, 1, 2 * tm), lambda i: (i, 0, 0), memory_space=pltpu.SMEM),
                  pl.BlockSpec((tm, D_MODEL), lambda i: (i, 0)),
                  pl.BlockSpec((tm, REC_WIDTH), lambda i: (i, 0)),
                  pl.BlockSpec(memory_space=pl.ANY)],
        out_specs=pl.BlockSpec((tm, D_MODEL), lambda i: (i, 0)),
        out_shape=jax.ShapeDtypeStruct((t, D_MODEL), F32),
        scratch_shapes=[pltpu.VMEM((tm, HALF_D), U32), pltpu.VMEM((tm, HALF_D), U32),
                        pltpu.SemaphoreType.DMA((1,))],
        compiler_params=_cparams(("arbitrary",)),
        name="moe_combine",
    )(pos, h, rec, out_slots)


def _moe(h, p):
    t = h.shape[0]
    xp, rec, cnt = _router(h, p["g"], p["w_r_hi"], p["w_r_lo"], p["b_r"])

    counts = cnt[0, :N_EXPERTS].astype(I32)
    blocks_per_expert = (counts + MOE_BLOCK - 1) // MOE_BLOCK
    block_end = jnp.cumsum(blocks_per_expert)
    slot_base = (block_end - blocks_per_expert) * MOE_BLOCK
    n_slots = (2 * t // MOE_BLOCK + N_EXPERTS) * MOE_BLOCK
    n_blocks = n_slots // MOE_BLOCK
    block_expert = jnp.minimum(jnp.searchsorted(block_end, jnp.arange(n_blocks, dtype=I32), side="right"),
                               N_EXPERTS - 1).astype(I32)
    n_used = block_end[-1:].astype(I32)
    e0 = rec[:, REC_E0].astype(I32)
    e1 = rec[:, REC_E1].astype(I32)
    pos0 = slot_base[e0] + rec[:, REC_R0].astype(I32)
    pos1 = slot_base[e1] + rec[:, REC_R1].astype(I32)
    tm = _row_tile(t, 256)
    pos = jnp.concatenate([pos0.reshape(t // tm, 1, tm), pos1.reshape(t // tm, 1, tm)], axis=2)

    xs = _dispatch(pos, xp, n_slots)
    out_slots = _experts(block_expert, n_used, xs, p["w_gu"], p["w_down"])
    return _combine(pos, h, rec, out_slots)


def _fox_in_kernel(h_ref, gkv_ref, gmix_ref, wk_ref, wv_ref, wf_ref, bf_ref, wq_ref, wg_ref,
                   gk_ref, gq_ref, bd_ref, k_ref, v_ref, lf_ref, q_ref, sg_ref, *, tm):
    x = h_ref[...]
    xn = x * _rms_scale(x)
    x_kv = (xn * gkv_ref[...]).astype(BF16)
    x_q = (xn * gmix_ref[...]).astype(BF16)
    bd = bd_ref[...]

    def head_norm(y):
        ms = jnp.dot((y * y).astype(BF16), bd, preferred_element_type=F32) * (1.0 / FOX_DH)
        return y * lax.rsqrt(ms + EPS)

    k = jnp.dot(x_kv, wk_ref[...], preferred_element_type=F32)
    k_ref[...] = head_norm(k) * gk_ref[...]
    v_ref[...] = jnp.dot(x_kv, wv_ref[...], preferred_element_type=F32)
    fl = jnp.dot(x_kv, wf_ref[...], preferred_element_type=F32) + bf_ref[...]
    lane = lax.broadcasted_iota(I32, (tm, LANES), 1)
    lf_ref[...] = jnp.where(lane < FOX_HEADS, jax.nn.log_sigmoid(fl), 0.0)
    q = jnp.dot(x_q, wq_ref[...], preferred_element_type=F32)
    q_ref[...] = (head_norm(q) * gq_ref[...]).astype(BF16)
    gate = jnp.dot(x_q, wg_ref[...], preferred_element_type=F32)
    sg_ref[...] = jax.nn.sigmoid(gate).astype(BF16)


def _fox_in_proj(h, p):
    t = h.shape[0]
    tm = _row_tile(t, 256)
    row = lambda n: pl.BlockSpec((tm, n), lambda i: (i, 0))
    full = lambda a: pl.BlockSpec(a.shape, lambda i: (0,) * a.ndim)
    consts = (p["g_kv"], p["g_mix"], p["w_k"], p["w_v"], p["w_f"], p["b_f"], p["w_q"], p["w_g"],
              p["g_k"], p["g_q"], p["head_sum"])
    return pl.pallas_call(
        functools.partial(_fox_in_kernel, tm=tm),
        grid=(t // tm,),
        in_specs=[row(D_MODEL)] + [full(a) for a in consts],
        out_specs=[row(D_MODEL), row(D_MODEL), row(LANES), row(D_MODEL), row(D_MODEL)],
        out_shape=[jax.ShapeDtypeStruct((t, D_MODEL), F32), jax.ShapeDtypeStruct((t, D_MODEL), F32),
                   jax.ShapeDtypeStruct((t, LANES), F32), jax.ShapeDtypeStruct((t, D_MODEL), BF16),
                   jax.ShapeDtypeStruct((t, D_MODEL), BF16)],
        compiler_params=_cparams(("parallel",)),
        name="fox_in_proj",
    )(h, *consts)


CUMSUM_ROWS = 128


def _cumsum_kernel(x_ref, c_ref, *, seq):
    r_i = lax.broadcasted_iota(I32, (CUMSUM_ROWS, CUMSUM_ROWS), 0)
    c_i = lax.broadcasted_iota(I32, (CUMSUM_ROWS, CUMSUM_ROWS), 1)
    tri = jnp.where(c_i <= r_i, 1.0, 0.0).astype(BF16)

    def body(g, carry):
        rows = pl.ds(pl.multiple_of(g * CUMSUM_ROWS, CUMSUM_ROWS), CUMSUM_ROWS)
        x = x_ref[0, rows, :]
        x1 = x.astype(BF16)
        rem = x - x1.astype(F32)
        x2 = rem.astype(BF16)
        x3 = (rem - x2.astype(F32)).astype(BF16)
        c = (jnp.dot(tri, x1, preferred_element_type=F32) + jnp.dot(tri, x2, preferred_element_type=F32)
             + jnp.dot(tri, x3, preferred_element_type=F32)) + carry
        c_ref[0, rows, :] = c
        return c[CUMSUM_ROWS - 1:CUMSUM_ROWS, :]

    lax.fori_loop(0, seq // CUMSUM_ROWS, body, jnp.zeros((1, LANES), F32))


def _cumsum(x):
    batch, seq, _ = x.shape
    assert seq % CUMSUM_ROWS == 0
    spec = pl.BlockSpec((1, seq, LANES), lambda b: (b, 0, 0))
    return pl.pallas_call(
        functools.partial(_cumsum_kernel, seq=seq),
        grid=(batch,),
        in_specs=[spec],
        out_specs=spec,
        out_shape=jax.ShapeDtypeStruct(x.shape, F32),
        compiler_params=_cparams(("parallel",)),
        name="logf_cumsum",
    )(x)


def _fox_attn_kernel(q_ref, k_ref, v_ref, c_ref, sg_ref, o_ref, kb_s, vb_s, ct_s, m_s, l_s, acc_s,
                     *, tq, tk, n_past, n_keys):
    hp = pl.program_id(1)
    qi = pl.program_id(2)

    @pl.when(qi == 0)
    def _():
        for hh in range(2):
            lanes = slice(hh * FOX_DH, (hh + 1) * FOX_DH)
            kb_s[hh] = k_ref[:, lanes].astype(BF16)
            vb_s[hh] = v_ref[:, lanes].astype(BF16)

    @pl.when((qi == 0) & (hp == 0))
    def _():
        ct = c_ref[0].T
        for head in range(FOX_HEADS):
            ct_s[head] = ct[head:head + 1, :]

    q_pos0 = n_past + qi * tq
    c_q_all = c_ref[0, pl.ds(pl.multiple_of(q_pos0, tq), tq), :]
    lane = lax.broadcasted_iota(I32, (tq, LANES), 1)
    q_all = q_ref[...]
    qs, cqs = [], []
    for hh in range(2):
        head = 2 * hp + hh
        qs.append(q_all[:, hh * FOX_DH:(hh + 1) * FOX_DH])
        cqs.append(jnp.sum(jnp.where(lane == head, c_q_all, 0.0), axis=1, keepdims=True))
        m_s[hh] = jnp.full((tq, 1), -jnp.inf, F32)
        l_s[hh] = jnp.zeros((tq, 1), F32)
        acc_s[hh] = jnp.zeros((tq, FOX_DH), F32)

    def block(j, masked):
        k0 = pl.multiple_of(j * tk, tk)
        for hh in range(2):
            head = 2 * hp + hh
            kh = kb_s[hh, pl.ds(k0, tk), :]
            vh = vb_s[hh, pl.ds(k0, tk), :]
            s = lax.dot_general(qs[hh], kh, (((1,), (1,)), ((), ())), preferred_element_type=F32)
            s = s + (cqs[hh] - ct_s[head, :, pl.ds(k0, tk)])
            if masked:
                k_pos = k0 + lax.broadcasted_iota(I32, (tq, tk), 1)
                q_pos = q_pos0 + lax.broadcasted_iota(I32, (tq, tk), 0)
                s = jnp.where(k_pos <= q_pos, s, -jnp.inf)
            m_old = m_s[hh]
            m_new = jnp.maximum(m_old, jnp.max(s, axis=1, keepdims=True))
            alpha = jnp.exp(m_old - m_new)
            p = jnp.exp(s - m_new)
            l_s[hh] = alpha * l_s[hh] + jnp.sum(p, axis=1, keepdims=True)
            acc_s[hh] = alpha * acc_s[hh] + jnp.dot(p.astype(BF16), vh, preferred_element_type=F32)
            m_s[hh] = m_new

    n_full = (q_pos0 + 1) // tk
    n_end = (q_pos0 + tq + tk - 1) // tk

    def full_body(j, carry):
        block(j, False)
        return carry

    def edge_body(j, carry):
        block(j, True)
        return carry

    lax.fori_loop(0, n_full, full_body, 0)
    lax.fori_loop(n_full, n_end, edge_body, 0)

    sg = sg_ref[...].astype(F32)
    for hh in range(2):
        lanes = slice(hh * FOX_DH, (hh + 1) * FOX_DH)
        o_ref[:, lanes] = (acc_s[hh] / l_s[hh] * sg[:, lanes]).astype(BF16)


def _fox_attention(q, sg, k_all, v_all, c_all, batch, seq_q, n_keys, n_past, tq, tk):
    assert seq_q % tq == 0 and n_keys % tk == 0 and n_past % tq == 0
    nq = seq_q // tq
    kern = functools.partial(_fox_attn_kernel, tq=tq, tk=tk, n_past=n_past, n_keys=n_keys)
    return pl.pallas_call(
        kern,
        grid=(batch, FOX_PAIRS, nq),
        in_specs=[pl.BlockSpec((tq, LANES), lambda b, hp, qi: (b * nq + qi, hp)),
                  pl.BlockSpec((n_keys, LANES), lambda b, hp, qi: (b, hp)),
                  pl.BlockSpec((n_keys, LANES), lambda b, hp, qi: (b, hp)),
                  pl.BlockSpec((1, n_keys, LANES), lambda b, hp, qi: (b, 0, 0)),
                  pl.BlockSpec((tq, LANES), lambda b, hp, qi: (b * nq + qi, hp))],
        out_specs=pl.BlockSpec((tq, LANES), lambda b, hp, qi: (b * nq + qi, hp)),
        out_shape=jax.ShapeDtypeStruct((batch * seq_q, D_MODEL), BF16),
        scratch_shapes=[pltpu.VMEM((2, n_keys, FOX_DH), BF16),
                        pltpu.VMEM((2, n_keys, FOX_DH), BF16),
                        pltpu.VMEM((FOX_HEADS, 1, n_keys), F32),
                        pltpu.VMEM((2, tq, 1), F32),
                        pltpu.VMEM((2, tq, 1), F32),
                        pltpu.VMEM((2, tq, FOX_DH), F32)],
        compiler_params=_cparams(("arbitrary", "arbitrary", "arbitrary")),
        name="fox_attention",
    )(q, k_all, v_all, c_all, sg)


def _pad_cols(w, n):
    return jnp.pad(w, ((0, 0), (0, n - w.shape[1])))


def _prepare(norm_mix, norm_ffn, w_gla_in, w_gla_gate2, b_gla_gate, g_gla_head, w_gla_out, g_kv, w_kv,
             b_forget, g_k, w_fox_qg, g_q, w_fox_out, w_group, b_group, w_router, b_router,
             w_exp_gate, w_exp_up, w_exp_down):
    row = lambda a: a.reshape(1, -1).astype(F32)
    w_in = w_gla_in[0]
    c0, c1, c2, c3 = 2 * GLA_QK, 2 * GLA_QK + GLA_V, 2 * GLA_QK + GLA_V + GLA_GATE_RANK, w_in.shape[1]
    gla = dict(
        g=row(norm_mix[0]),
        w_qk=w_in[:, :c0].astype(BF16), w_v=w_in[:, c0:c1].astype(BF16), w_r=w_in[:, c2:c3].astype(BF16),
        w_g=_pad_cols(w_in[:, c1:c2], LANES).astype(BF16),
        w_g2=jnp.pad(w_gla_gate2[0], ((0, LANES - GLA_GATE_RANK), (0, 0))).astype(BF16),
        b_g=row(b_gla_gate[0]), g_head=row(g_gla_head[0]), w_out=w_gla_out[0].astype(BF16),
    )
    head_id = jnp.arange(D_MODEL) // FOX_DH
    fox = dict(
        g_kv=row(g_kv), g_mix=row(norm_mix[1]),
        w_k=w_kv[:, :D_MODEL].astype(BF16), w_v=w_kv[:, D_MODEL:2 * D_MODEL].astype(BF16),
        w_f=_pad_cols(w_kv[:, 2 * D_MODEL:], LANES).astype(BF16),
        b_f=_pad_cols(row(b_forget), LANES),
        w_q=w_fox_qg[0][:, :D_MODEL].astype(BF16), w_g=w_fox_qg[0][:, D_MODEL:].astype(BF16),
        g_k=row(jnp.tile(g_k, FOX_HEADS)), g_q=row(jnp.tile(g_q[0], FOX_HEADS)) * (FOX_DH ** -0.5),
        head_sum=(head_id[:, None] == head_id[None, :]).astype(BF16),
        w_out=w_fox_out[0].astype(BF16),
    )
    moe = []
    for layer in range(2):
        w_r = _pad_cols(jnp.concatenate([w_group[layer], w_router[layer]], axis=1), LANES)
        w_r_hi = w_r.astype(BF16)
        moe.append(dict(
            g=row(norm_ffn[layer]),
            w_r_hi=w_r_hi, w_r_lo=(w_r - w_r_hi.astype(F32)).astype(BF16),
            b_r=_pad_cols(jnp.concatenate([row(b_group[layer]), row(b_router[layer])], axis=1), LANES),
            w_gu=jnp.concatenate([w_exp_gate[layer], w_exp_up[layer]], axis=2).astype(BF16),
            w_down=w_exp_down[layer].astype(BF16),
        ))
    return gla, fox, moe


def _trunk(x, s0, past, gla, fox, moe):
    batch, seq, _ = x.shape
    t = batch * seq
    h = x.reshape(t, D_MODEL)

    q, k, v, r, la = _gla_in_proj(h, gla["g"], gla["w_qk"], gla["w_v"], gla["w_r"], gla["w_g"], gla["w_g2"],
                                  gla["b_g"])
    o, s_new = _gla(q, k, v, la, s0, gla["g_head"], batch, seq)
    h = _out_proj(o, r, gla["w_out"], h)
    h = _moe(h, moe[0])

    k_new, v_new, lf_new, qf, sg = _fox_in_proj(h, fox)
    lf3 = lf_new.reshape(batch, seq, LANES)
    if past is None:
        n_past, n_keys = 0, seq
        tq = tk = min(seq, 128)
        k_all, v_all, lf_all = k_new, v_new, lf3
    else:
        past_k, past_v, past_lf = past
        n_past = past_k.shape[1]
        tq, tk = seq, 128
        n_keys = -(-(n_past + seq) // tk) * tk
        pad = n_keys - n_past - seq
        cat = lambda a, b: jnp.concatenate(
            [a, b, jnp.zeros((batch, pad, a.shape[2]), F32)], axis=1)
        k_all = cat(past_k.reshape(batch, n_past, D_MODEL), k_new.reshape(batch, seq, D_MODEL)
                    ).reshape(batch * n_keys, D_MODEL)
        v_all = cat(past_v.reshape(batch, n_past, D_MODEL), v_new.reshape(batch, seq, D_MODEL)
                    ).reshape(batch * n_keys, D_MODEL)
        lf_all = cat(jnp.pad(past_lf, ((0, 0), (0, 0), (0, LANES - FOX_HEADS))), lf3)
    c_all = _cumsum(lf_all)
    o = _fox_attention(qf, sg, k_all, v_all, c_all, batch, seq, n_keys, n_past, tq, tk)
    h = _out_proj(o, None, fox["w_out"], h)
    h = _moe(h, moe[1])

    return (h.reshape(batch, seq, D_MODEL),
            k_new.reshape(batch, seq, FOX_HEADS, FOX_DH),
            v_new.reshape(batch, seq, FOX_HEADS, FOX_DH),
            lf3[:, :, :FOX_HEADS],
            s_new[None])


def kernel(x_prompt, x_sample, cache_k, cache_v, cache_logf, state_gla, norm_mix, norm_ffn, w_gla_in, w_gla_gate2, b_gla_gate, g_gla_head, w_gla_out, g_kv, w_kv, b_forget, g_k, w_fox_qg, g_q, w_fox_out, w_group, b_group, w_router, b_router, w_exp_gate, w_exp_up, w_exp_down):
    gla, fox, moe = _prepare(norm_mix, norm_ffn, w_gla_in, w_gla_gate2, b_gla_gate, g_gla_head, w_gla_out, g_kv,
                             w_kv, b_forget, g_k, w_fox_qg, g_q, w_fox_out, w_group, b_group, w_router, b_router,
                             w_exp_gate, w_exp_up, w_exp_down)
    s_zero = jnp.zeros((x_prompt.shape[0], GLA_HEADS, GLA_DK, GLA_DV), F32)
    y_p, k_p, v_p, lf_p, s_p = _trunk(x_prompt, s_zero, None, gla, fox, moe)
    y_s, k_s, v_s, lf_s, s_s = _trunk(x_sample, state_gla[0].astype(F32),
                                      (cache_k.astype(F32), cache_v.astype(F32), cache_logf.astype(F32)),
                                      gla, fox, moe)
    return (y_p, y_s, k_p, v_p, lf_p, s_p, k_s, v_s, lf_s, s_s)
```

```python
import functools

import jax
import jax.numpy as jnp
from jax import lax
from jax.experimental import pallas as pl
from jax.experimental.pallas import tpu as pltpu

F32 = jnp.float32
BF16 = jnp.bfloat16
U32 = jnp.uint32
I32 = jnp.int32

EPS = 1e-6
D_MODEL = 1024
HALF_D = D_MODEL // 2
LANES = 128
VMEM_LIMIT_BYTES = 56 * 1024 * 1024

GLA_HEADS = 4
GLA_DK = 128
GLA_DV = 256
GLA_QK = GLA_HEADS * GLA_DK
GLA_V = GLA_HEADS * GLA_DV
GLA_GATE_RANK = 16
GLA_GATE_TAU = 16.0
GLA_CHUNK = 64
GLA_GROUP = 256

FOX_HEADS = 16
FOX_DH = 64
FOX_PAIRS = FOX_HEADS // 2

N_GROUPS = 4
EXPERTS_PER_GROUP = 8
N_EXPERTS = N_GROUPS * EXPERTS_PER_GROUP
D_EXPERT = 512
MOE_BLOCK = 256
ROW_DMA_UNROLL = 8


def _cparams(sem):
    return pltpu.CompilerParams(dimension_semantics=sem, vmem_limit_bytes=VMEM_LIMIT_BYTES)


def _row_tile(t, pref):
    tm = min(t, pref)
    assert t % tm == 0
    return tm


def _rms_scale(x):
    return lax.rsqrt(jnp.mean(x * x, axis=-1, keepdims=True) + EPS)


def _pack_halves(y):
    lo = lax.bitcast_convert_type(y[:, :HALF_D].astype(BF16).astype(F32), U32)
    hi = lax.bitcast_convert_type(y[:, HALF_D:].astype(BF16).astype(F32), U32)
    return (lo >> 16) | (hi & jnp.uint32(0xFFFF0000))


def _unpack_halves(p):
    lo = lax.bitcast_convert_type(p << 16, F32)
    hi = lax.bitcast_convert_type(p & jnp.uint32(0xFFFF0000), F32)
    return lo, hi


def _gla_in_kernel(h_ref, g_ref, wqk_ref, wv_ref, wr_ref, wg_ref, wg2_ref, bg_ref,
                   q_ref, k_ref, v_ref, r_ref, la_ref):
    x = h_ref[...]
    hn = (x * _rms_scale(x) * g_ref[...]).astype(BF16)
    qk = jnp.dot(hn, wqk_ref[...], preferred_element_type=F32)
    q_ref[...] = (qk[:, :GLA_QK] * (GLA_DK ** -0.5)).astype(BF16)
    k_ref[...] = qk[:, GLA_QK:].astype(BF16)
    v_ref[...] = jnp.dot(hn, wv_ref[...], preferred_element_type=F32).astype(BF16)
    r_ref[...] = jnp.dot(hn, wr_ref[...], preferred_element_type=F32).astype(BF16)
    g_lr = jnp.dot(hn, wg_ref[...], preferred_element_type=F32)
    z = jnp.dot(g_lr.astype(BF16), wg2_ref[...], preferred_element_type=F32) + bg_ref[...]
    la_ref[...] = jax.nn.log_sigmoid(z) / GLA_GATE_TAU


def _gla_in_proj(h, g, wqk, wv, wr, wg, wg2, bg):
    t = h.shape[0]
    tm = _row_tile(t, 512)
    row = lambda n: pl.BlockSpec((tm, n), lambda i: (i, 0))
    full = lambda a: pl.BlockSpec(a.shape, lambda i: (0,) * a.ndim)
    return pl.pallas_call(
        _gla_in_kernel,
        grid=(t // tm,),
        in_specs=[row(D_MODEL), full(g), full(wqk), full(wv), full(wr), full(wg), full(wg2), full(bg)],
        out_specs=[row(GLA_QK), row(GLA_QK), row(GLA_V), row(GLA_V), row(GLA_QK)],
        out_shape=[jax.ShapeDtypeStruct((t, GLA_QK), BF16), jax.ShapeDtypeStruct((t, GLA_QK), BF16),
                   jax.ShapeDtypeStruct((t, GLA_V), BF16), jax.ShapeDtypeStruct((t, GLA_V), BF16),
                   jax.ShapeDtypeStruct((t, GLA_QK), F32)],
        compiler_params=_cparams(("parallel",)),
        name="gla_in_proj",
    )(h, g, wqk, wv, wr, wg, wg2, bg)


def _gla_kernel(q_ref, k_ref, v_ref, la_ref, s0_ref, gh_ref, o_ref, s_ref,
                qb_s, oi_s, u_s, dl_s, *, seq, chunk, group):
    n_groups = seq // group
    n_chunks = seq // chunk
    per_group = group // chunk

    shift = chunk.bit_length() - 1
    assert chunk == 1 << shift and group & (group - 1) == 0
    row = lax.broadcasted_iota(I32, (group, group), 0)
    col = lax.broadcasted_iota(I32, (group, group), 1)
    tril = ((row >> shift) == (col >> shift)) & (col <= row)
    tril_bf = jnp.where(tril, 1.0, 0.0).astype(BF16)

    def chunk_rows(b, at):
        return jnp.concatenate(
            [jnp.broadcast_to(b[j * chunk + at:j * chunk + at + 1], (chunk, GLA_DK)) for j in range(per_group)],
            axis=0)

    def group_body(g, carry):
        r0 = pl.multiple_of(g * group, group)
        rows = pl.ds(r0, group)
        la = la_ref[rows, :]
        la_hi = la.astype(BF16)
        la_lo = (la - la_hi.astype(F32)).astype(BF16)
        b = (jnp.dot(tril_bf, la_hi, preferred_element_type=F32)
             + jnp.dot(tril_bf, la_lo, preferred_element_type=F32))
        b_ref = chunk_rows(b, chunk // 2)
        b_last = chunk_rows(b, chunk - 1)
        qf = q_ref[rows, :].astype(F32)
        kf = k_ref[rows, :].astype(F32)
        vg = v_ref[rows, :]
        qe = (qf * jnp.exp(b - b_ref)).astype(BF16)
        ke = (kf * jnp.exp(b_ref - b)).astype(BF16)
        att = lax.dot_general(qe, ke, (((1,), (1,)), ((), ())), preferred_element_type=F32)
        att = jnp.where(tril, att, 0.0).astype(BF16)
        oi_s[rows, :] = jnp.dot(att, vg, preferred_element_type=F32)
        qb_s[rows, :] = (qf * jnp.exp(b)).astype(BF16)
        kl = (kf * jnp.exp(b_last - b)).astype(BF16)
        decay = jnp.exp(b_last)
        for j in range(per_group):
            c = g * per_group + j
            sl = slice(j * chunk, (j + 1) * chunk)
            u_s[c] = lax.dot_general(vg[sl], kl[sl], (((0,), (0,)), ((), ())), preferred_element_type=F32)
            dl_s[c] = decay[j * chunk:j * chunk + 1]
        return carry

    lax.fori_loop(0, n_groups, group_body, 0, unroll=min(2, n_groups))

    gh = gh_ref[...]

    def chunk_body(c, st):
        r0 = pl.multiple_of(c * chunk, chunk)
        rows = pl.ds(r0, chunk)
        o = oi_s[rows, :] + lax.dot_general(qb_s[rows, :], st.astype(BF16), (((1,), (1,)), ((), ())),
                                            preferred_element_type=F32)
        o_ref[rows, :] = (o * _rms_scale(o) * gh).astype(BF16)
        return st * dl_s[c] + u_s[c]

    st = lax.fori_loop(0, n_chunks, chunk_body, s0_ref[0, 0].T, unroll=min(4, n_chunks))
    s_ref[0, 0] = st.T


def _gla(q, k, v, la, s0, g_head, batch, seq):
    chunk = min(seq, GLA_CHUNK)
    group = min(seq, GLA_GROUP)
    assert seq % group == 0 and group % chunk == 0
    n_chunks = seq // chunk
    kern = functools.partial(_gla_kernel, seq=seq, chunk=chunk, group=group)
    return pl.pallas_call(
        kern,
        grid=(batch, GLA_HEADS),
        in_specs=[pl.BlockSpec((seq, GLA_DK), lambda b, h: (b, h)),
                  pl.BlockSpec((seq, GLA_DK), lambda b, h: (b, h)),
                  pl.BlockSpec((seq, GLA_DV), lambda b, h: (b, h)),
                  pl.BlockSpec((seq, GLA_DK), lambda b, h: (b, h)),
                  pl.BlockSpec((1, 1, GLA_DK, GLA_DV), lambda b, h: (b, h, 0, 0)),
                  pl.BlockSpec((1, GLA_DV), lambda b, h: (0, 0))],
        out_specs=[pl.BlockSpec((seq, GLA_DV), lambda b, h: (b, h)),
                   pl.BlockSpec((1, 1, GLA_DK, GLA_DV), lambda b, h: (b, h, 0, 0))],
        out_shape=[jax.ShapeDtypeStruct((batch * seq, GLA_V), BF16),
                   jax.ShapeDtypeStruct((batch, GLA_HEADS, GLA_DK, GLA_DV), F32)],
        scratch_shapes=[pltpu.VMEM((seq, GLA_DK), BF16),
                        pltpu.VMEM((seq, GLA_DV), F32),
                        pltpu.VMEM((n_chunks, GLA_DV, GLA_DK), F32),
                        pltpu.VMEM((n_chunks, 1, GLA_DK), F32)],
        compiler_params=_cparams(("parallel", "parallel")),
        name="gla_scan",
    )(q, k, v, la, s0, g_head)


def _out_proj_kernel(*refs, gated):
    if gated:
        a_ref, r_ref, w_ref, h_ref, o_ref = refs
        r = r_ref[...].astype(F32)
        a = (a_ref[...].astype(F32) * (r * jax.nn.sigmoid(r))).astype(BF16)
    else:
        a_ref, w_ref, h_ref, o_ref = refs
        a = a_ref[...]
    o_ref[...] = h_ref[...] + jnp.dot(a, w_ref[...], preferred_element_type=F32)


def _out_proj(a, r, w, h):
    t = h.shape[0]
    tm = _row_tile(t, 512)
    row = pl.BlockSpec((tm, D_MODEL), lambda i: (i, 0))
    wspec = pl.BlockSpec(w.shape, lambda i: (0, 0))
    gated = r is not None
    args = (a, r, w, h) if gated else (a, w, h)
    specs = [row, row, wspec, row] if gated else [row, wspec, row]
    return pl.pallas_call(
        functools.partial(_out_proj_kernel, gated=gated),
        grid=(t // tm,),
        in_specs=specs,
        out_specs=row,
        out_shape=jax.ShapeDtypeStruct((t, D_MODEL), F32),
        compiler_params=_cparams(("parallel",)),
        name="out_proj_gated" if gated else "out_proj",
    )(*args)


REC_E0, REC_E1, REC_W0, REC_W1, REC_R0, REC_R1 = range(6)
REC_WIDTH = 8


def _router_kernel(h_ref, g_ref, whi_ref, wlo_ref, bias_ref, xp_ref, rec_ref, rec_t_ref, cnt_ref, carry_s, *, tm):
    @pl.when(pl.program_id(0) == 0)
    def _():
        carry_s[...] = jnp.zeros_like(carry_s)

    x = h_ref[...]
    hn = x * _rms_scale(x) * g_ref[...]
    xp_ref[...] = _pack_halves(hn)

    hi = hn.astype(BF16)
    lo = (hn - hi.astype(F32)).astype(BF16)
    w_hi = whi_ref[...]
    lg = (jnp.dot(hi, w_hi, preferred_element_type=F32) + jnp.dot(lo, w_hi, preferred_element_type=F32)
          + jnp.dot(hi, wlo_ref[...], preferred_element_type=F32) + bias_ref[...])

    lane = lax.broadcasted_iota(I32, (tm, LANES), 1).astype(F32)
    neg = jnp.float32(-jnp.inf)

    def masked_softmax(mask):
        m = jnp.max(jnp.where(mask, lg, neg), axis=1, keepdims=True)
        e = jnp.where(mask, jnp.exp(lg - m), 0.0)
        return e / jnp.sum(e, axis=1, keepdims=True)

    def top1(p, mask):
        v = jnp.max(jnp.where(mask, p, -1.0), axis=1, keepdims=True)
        idx = jnp.min(jnp.where(mask & (p == v), lane, float(LANES)), axis=1, keepdims=True)
        return v, idx

    gmask = lane < N_GROUPS
    g_top, g_idx = top1(masked_softmax(gmask), gmask)
    e_lo = N_GROUPS + EXPERTS_PER_GROUP * g_idx
    emask = (lane >= e_lo) & (lane < e_lo + EXPERTS_PER_GROUP)
    ep = masked_softmax(emask)
    p1, i1 = top1(ep, emask)
    mask2 = emask & (lane != i1)
    p2, i2 = top1(ep, mask2)
    e0 = i1 - N_GROUPS
    e1 = i2 - N_GROUPS
    denom = p1 + p2
    w0 = g_top * p1 / denom
    w1 = g_top * p2 / denom

    hit0 = lane == e0
    hit1 = lane == e1
    onehot = jnp.where(hit0 | hit1, 1.0, 0.0)
    r_i = lax.broadcasted_iota(I32, (tm, tm), 0)
    c_i = lax.broadcasted_iota(I32, (tm, tm), 1)
    before = jnp.where(c_i < r_i, 1.0, 0.0).astype(BF16)
    prior = jnp.dot(before, onehot.astype(BF16), preferred_element_type=F32) + carry_s[...]
    rank0 = jnp.sum(jnp.where(hit0, prior, 0.0), axis=1, keepdims=True)
    rank1 = jnp.sum(jnp.where(hit1, prior, 0.0), axis=1, keepdims=True)
    carry_s[...] = carry_s[...] + jnp.sum(onehot, axis=0, keepdims=True)
    cnt_ref[...] = carry_s[...]

    rec = jnp.zeros((tm, LANES), F32)
    for slot, val in ((REC_E0, e0), (REC_E1, e1), (REC_W0, w0), (REC_W1, w1), (REC_R0, rank0), (REC_R1, rank1)):
        rec = jnp.where(lane == slot, val, rec)
    rec_ref[...] = rec[:, :REC_WIDTH]
    rec_t_ref[...] = rec.T[:REC_WIDTH]


def _router(h, g, w_hi, w_lo, bias):
    t = h.shape[0]
    tm = _row_tile(t, 512)
    full = lambda a: pl.BlockSpec(a.shape, lambda i: (0,) * a.ndim)
    return pl.pallas_call(
        functools.partial(_router_kernel, tm=tm),
        grid=(t // tm,),
        in_specs=[pl.BlockSpec((tm, D_MODEL), lambda i: (i, 0)), full(g), full(w_hi), full(w_lo), full(bias)],
        out_specs=[pl.BlockSpec((tm, HALF_D), lambda i: (i, 0)),
                   pl.BlockSpec((tm, REC_WIDTH), lambda i: (i, 0)),
                   pl.BlockSpec((REC_WIDTH, tm), lambda i: (0, i)),
                   pl.BlockSpec((1, LANES), lambda i: (0, 0))],
        out_shape=[jax.ShapeDtypeStruct((t, HALF_D), U32),
                   jax.ShapeDtypeStruct((t, REC_WIDTH), F32),
                   jax.ShapeDtypeStruct((REC_WIDTH, t), F32),
                   jax.ShapeDtypeStruct((1, LANES), F32)],
        scratch_shapes=[pltpu.VMEM((1, LANES), F32)],
        compiler_params=_cparams(("arbitrary",)),
        name="moe_router",
    )(h, g, w_hi, w_lo, bias)


def _row_copy(src, src_row, dst, dst_row, sem):
    return pltpu.make_async_copy(src.at[pl.ds(src_row, 1)], dst.at[pl.ds(dst_row, 1)], sem)


def _dispatch_kernel(pos_ref, xp_ref, zeros_hbm, xs_hbm, sem, *, tm):
    del zeros_hbm

    def issue(r, carry):
        _row_copy(xp_ref, r, xs_hbm, pos_ref[0, 0, r], sem.at[0]).start()
        _row_copy(xp_ref, r, xs_hbm, pos_ref[0, 0, tm + r], sem.at[0]).start()
        return carry

    lax.fori_loop(0, tm, issue, 0, unroll=ROW_DMA_UNROLL)

    def drain(r, carry):
        _row_copy(xp_ref, 0, xs_hbm, 0, sem.at[0]).wait()
        _row_copy(xp_ref, 0, xs_hbm, 0, sem.at[0]).wait()
        return carry

    lax.fori_loop(0, tm, drain, 0, unroll=ROW_DMA_UNROLL)


def _dispatch(pos, xp, n_slots):
    t = xp.shape[0]
    tm = pos.shape[2] // 2
    return pl.pallas_call(
        functools.partial(_dispatch_kernel, tm=tm),
        grid=(t // tm,),
        in_specs=[pl.BlockSpec((1, 1, 2 * tm), lambda i: (i, 0, 0), memory_space=pltpu.SMEM),
                  pl.BlockSpec((tm, HALF_D), lambda i: (i, 0)),
                  pl.BlockSpec(memory_space=pl.ANY)],
        out_specs=pl.BlockSpec(memory_space=pl.ANY),
        out_shape=jax.ShapeDtypeStruct((n_slots, HALF_D), U32),
        input_output_aliases={2: 0},
        scratch_shapes=[pltpu.SemaphoreType.DMA((1,))],
        compiler_params=_cparams(("arbitrary",)),
        name="moe_dispatch",
    )(pos, xp, jnp.zeros((n_slots, HALF_D), U32))


def _expert_kernel(be_ref, nb_ref, xs_ref, wgu_ref, wd_ref, out_ref):
    @pl.when(pl.program_id(0) < nb_ref[0])
    def _():
        lo, hi = _unpack_halves(xs_ref[...])
        gu = (jnp.dot(lo.astype(BF16), wgu_ref[0, :HALF_D, :], preferred_element_type=F32)
              + jnp.dot(hi.astype(BF16), wgu_ref[0, HALF_D:, :], preferred_element_type=F32))
        gate = gu[:, :D_EXPERT]
        hid = (gate * jax.nn.sigmoid(gate) * gu[:, D_EXPERT:]).astype(BF16)
        out_ref[...] = _pack_halves(jnp.dot(hid, wd_ref[0], preferred_element_type=F32))

    @pl.when(pl.program_id(0) >= nb_ref[0])
    def _():
        out_ref[...] = jnp.zeros_like(out_ref)


def _experts(block_expert, n_blocks_used, xs, w_gu, w_down):
    n_slots = xs.shape[0]
    n_blocks = n_slots // MOE_BLOCK

    def used(b, nb):
        return jnp.minimum(b, nb[0] - 1)

    grid_spec = pltpu.PrefetchScalarGridSpec(
        num_scalar_prefetch=2,
        grid=(n_blocks,),
        in_specs=[pl.BlockSpec((MOE_BLOCK, HALF_D), lambda b, be, nb: (used(b, nb), 0)),
                  pl.BlockSpec((1, D_MODEL, 2 * D_EXPERT), lambda b, be, nb: (be[used(b, nb)], 0, 0)),
                  pl.BlockSpec((1, D_EXPERT, D_MODEL), lambda b, be, nb: (be[used(b, nb)], 0, 0))],
        out_specs=pl.BlockSpec((MOE_BLOCK, HALF_D), lambda b, be, nb: (b, 0)),
    )
    return pl.pallas_call(
        _expert_kernel,
        grid_spec=grid_spec,
        out_shape=jax.ShapeDtypeStruct((n_slots, HALF_D), U32),
        compiler_params=_cparams(("arbitrary",)),
        name="moe_experts",
    )(block_expert, n_blocks_used, xs, w_gu, w_down)


def _combine_kernel(pos_ref, h_ref, rec_ref, out_hbm, y_ref, g0_s, g1_s, sem, *, tm):
    def issue(r, carry):
        _row_copy(out_hbm, pos_ref[0, 0, r], g0_s, r, sem.at[0]).start()
        _row_copy(out_hbm, pos_ref[0, 0, tm + r], g1_s, r, sem.at[0]).start()
        return carry

    lax.fori_loop(0, tm, issue, 0, unroll=ROW_DMA_UNROLL)

    def drain(r, carry):
        _row_copy(out_hbm, 0, g0_s, 0, sem.at[0]).wait()
        _row_copy(out_hbm, 0, g1_s, 0, sem.at[0]).wait()
        return carry

    lax.fori_loop(0, tm, drain, 0, unroll=ROW_DMA_UNROLL)

    rec = rec_ref[...]
    w0 = rec[:, REC_W0:REC_W0 + 1]
    w1 = rec[:, REC_W1:REC_W1 + 1]
    lo0, hi0 = _unpack_halves(g0_s[...])
    lo1, hi1 = _unpack_halves(g1_s[...])
    y_ref[:, :HALF_D] = h_ref[:, :HALF_D] + w0 * lo0 + w1 * lo1
    y_ref[:, HALF_D:] = h_ref[:, HALF_D:] + w0 * hi0 + w1 * hi1


def _combine(pos, h, rec, out_slots):
    t = h.shape[0]
    tm = pos.shape[2] // 2
    return pl.pallas_call(
        functools.partial(_combine_kernel, tm=tm),
        grid=(t // tm,),
        in_specs=[pl.BlockSpec((1, 1, 2 * tm), lambda i: (i, 0, 0), memory_space=pltpu.SMEM),
                  pl.BlockSpec((tm, D_MODEL), lambda i: (i, 0)),
                  pl.BlockSpec((tm, REC_WIDTH), lambda i: (i, 0)),
                  pl.BlockSpec(memory_space=pl.ANY)],
        out_specs=pl.BlockSpec((tm, D_MODEL), lambda i: (i, 0)),
        out_shape=jax.ShapeDtypeStruct((t, D_MODEL), F32),
        scratch_shapes=[pltpu.VMEM((tm, HALF_D), U32), pltpu.VMEM((tm, HALF_D), U32),
                        pltpu.SemaphoreType.DMA((1,))],
        compiler_params=_cparams(("arbitrary",)),
        name="moe_combine",
    )(pos, h, rec, out_slots)


def _moe(h, p):
    t = h.shape[0]
    xp, rec, rec_t, cnt = _router(h, p["g"], p["w_r_hi"], p["w_r_lo"], p["b_r"])

    counts = cnt[0, :N_EXPERTS].astype(I32)
    blocks_per_expert = (counts + MOE_BLOCK - 1) // MOE_BLOCK
    block_end = jnp.cumsum(blocks_per_expert)
    slot_base = (block_end - blocks_per_expert) * MOE_BLOCK
    n_slots = (2 * t // MOE_BLOCK + N_EXPERTS) * MOE_BLOCK
    n_blocks = n_slots // MOE_BLOCK
    block_expert = jnp.minimum(
        jnp.sum(block_end[None, :] <= jnp.arange(n_blocks, dtype=I32)[:, None], axis=1), N_EXPERTS - 1).astype(I32)
    n_used = block_end[-1:].astype(I32)
    expert_ids = jnp.arange(N_EXPERTS, dtype=F32)[:, None]
    base_of = lambda e_row: jnp.sum(jnp.where(e_row[None, :] == expert_ids, slot_base[:, None], 0), axis=0)
    pos0 = base_of(rec_t[REC_E0]) + rec_t[REC_R0].astype(I32)
    pos1 = base_of(rec_t[REC_E1]) + rec_t[REC_R1].astype(I32)
    tm = _row_tile(t, 256)
    pos = jnp.concatenate([pos0.reshape(t // tm, 1, tm), pos1.reshape(t // tm, 1, tm)], axis=2)

    xs = _dispatch(pos, xp, n_slots)
    out_slots = _experts(block_expert, n_used, xs, p["w_gu"], p["w_down"])
    return _combine(pos, h, rec, out_slots)


def _fox_in_kernel(h_ref, gkv_ref, gmix_ref, wk_ref, wv_ref, wf_ref, bf_ref, wq_ref, wg_ref,
                   gk_ref, gq_ref, bd_ref, k_ref, v_ref, lf_ref, q_ref, sg_ref, *, tm):
    x = h_ref[...]
    xn = x * _rms_scale(x)
    x_kv = (xn * gkv_ref[...]).astype(BF16)
    x_q = (xn * gmix_ref[...]).astype(BF16)
    bd = bd_ref[...]

    def head_norm(y):
        ms = jnp.dot((y * y).astype(BF16), bd, preferred_element_type=F32) * (1.0 / FOX_DH)
        return y * lax.rsqrt(ms + EPS)

    k = jnp.dot(x_kv, wk_ref[...], preferred_element_type=F32)
    k_ref[...] = head_norm(k) * gk_ref[...]
    v_ref[...] = jnp.dot(x_kv, wv_ref[...], preferred_element_type=F32)
    fl = jnp.dot(x_kv, wf_ref[...], preferred_element_type=F32) + bf_ref[...]
    lane = lax.broadcasted_iota(I32, (tm, LANES), 1)
    lf_ref[...] = jnp.where(lane < FOX_HEADS, jax.nn.log_sigmoid(fl), 0.0)
    q = jnp.dot(x_q, wq_ref[...], preferred_element_type=F32)
    q_ref[...] = (head_norm(q) * gq_ref[...]).astype(BF16)
    gate = jnp.dot(x_q, wg_ref[...], preferred_element_type=F32)
    sg_ref[...] = jax.nn.sigmoid(gate).astype(BF16)


def _fox_in_proj(h, p):
    t = h.shape[0]
    tm = _row_tile(t, 256)
    row = lambda n: pl.BlockSpec((tm, n), lambda i: (i, 0))
    full = lambda a: pl.BlockSpec(a.shape, lambda i: (0,) * a.ndim)
    consts = (p["g_kv"], p["g_mix"], p["w_k"], p["w_v"], p["w_f"], p["b_f"], p["w_q"], p["w_g"],
              p["g_k"], p["g_q"], p["head_sum"])
    return pl.pallas_call(
        functools.partial(_fox_in_kernel, tm=tm),
        grid=(t // tm,),
        in_specs=[row(D_MODEL)] + [full(a) for a in consts],
        out_specs=[row(D_MODEL), row(D_MODEL), row(LANES), row(D_MODEL), row(D_MODEL)],
        out_shape=[jax.ShapeDtypeStruct((t, D_MODEL), F32), jax.ShapeDtypeStruct((t, D_MODEL), F32),
                   jax.ShapeDtypeStruct((t, LANES), F32), jax.ShapeDtypeStruct((t, D_MODEL), BF16),
                   jax.ShapeDtypeStruct((t, D_MODEL), BF16)],
        compiler_params=_cparams(("parallel",)),
        name="fox_in_proj",
    )(h, *consts)


CUMSUM_ROWS = 128


def _cumsum_kernel(x_ref, c_ref, *, seq):
    r_i = lax.broadcasted_iota(I32, (CUMSUM_ROWS, CUMSUM_ROWS), 0)
    c_i = lax.broadcasted_iota(I32, (CUMSUM_ROWS, CUMSUM_ROWS), 1)
    tri = jnp.where(c_i <= r_i, 1.0, 0.0).astype(BF16)

    def body(g, carry):
        rows = pl.ds(pl.multiple_of(g * CUMSUM_ROWS, CUMSUM_ROWS), CUMSUM_ROWS)
        x = x_ref[0, rows, :]
        x1 = x.astype(BF16)
        rem = x - x1.astype(F32)
        x2 = rem.astype(BF16)
        x3 = (rem - x2.astype(F32)).astype(BF16)
        c = (jnp.dot(tri, x1, preferred_element_type=F32) + jnp.dot(tri, x2, preferred_element_type=F32)
             + jnp.dot(tri, x3, preferred_element_type=F32)) + carry
        c_ref[0, rows, :] = c
        return c[CUMSUM_ROWS - 1:CUMSUM_ROWS, :]

    lax.fori_loop(0, seq // CUMSUM_ROWS, body, jnp.zeros((1, LANES), F32))


def _cumsum(x):
    batch, seq, _ = x.shape
    assert seq % CUMSUM_ROWS == 0
    spec = pl.BlockSpec((1, seq, LANES), lambda b: (b, 0, 0))
    return pl.pallas_call(
        functools.partial(_cumsum_kernel, seq=seq),
        grid=(batch,),
        in_specs=[spec],
        out_specs=spec,
        out_shape=jax.ShapeDtypeStruct(x.shape, F32),
        compiler_params=_cparams(("parallel",)),
        name="logf_cumsum",
    )(x)


LOG2E = 1.4426950408889634
KEY_ALIGN = 128


def _split3(x):
    x1 = x.astype(BF16).astype(F32)
    r = x - x1
    x2 = r.astype(BF16).astype(F32)
    return x1, x2, r - x2


def _fox_attn_kernel(q_ref, k_ref, v_ref, c_ref, sg_ref, o_ref, *, seq_q, tq, n_past, n_keys):
    hp = pl.program_id(1)
    lane_k = lax.broadcasted_iota(I32, (n_keys, LANES), 1)
    lane_q = lax.broadcasted_iota(I32, (seq_q, LANES), 1)
    c_pair = c_ref[0] * LOG2E
    k_pair = k_ref[...]
    v_pair = v_ref[...]
    q_pair = q_ref[...].astype(F32)

    k_aug, v_aug, q_aug, den_lane = [], [], [], []
    for hh in range(2):
        own_lo = hh * FOX_DH
        other = FOX_DH - own_lo
        c_k = jnp.sum(jnp.where(lane_k == 2 * hp + hh, c_pair, 0.0), axis=1, keepdims=True)
        c1, c2, c3 = _split3(c_k)
        d_k = lane_k - other
        extra_k = jnp.where((d_k >= 0) & (d_k < 3), 1.0,
                            jnp.where(d_k == 3, -c1, jnp.where(d_k == 4, -c2, jnp.where(d_k == 5, -c3, 0.0))))
        own_k = (lane_k >= own_lo) & (lane_k < own_lo + FOX_DH)
        k_aug.append(jnp.where(own_k, k_pair, extra_k).astype(BF16))
        v_aug.append(jnp.where(own_k, v_pair, jnp.where(d_k == 0, 1.0, 0.0)).astype(BF16))
        q1, q2, q3 = (c[n_past:n_past + seq_q] for c in (c1, c2, c3))
        d_q = lane_q - other
        extra_q = jnp.where(d_q == 0, q1, jnp.where(d_q == 1, q2, jnp.where(d_q == 2, q3,
                            jnp.where((d_q >= 3) & (d_q < 6), 1.0, 0.0))))
        own_q = (lane_q >= own_lo) & (lane_q < own_lo + FOX_DH)
        q_aug.append(jnp.where(own_q, q_pair, extra_q).astype(BF16))
        den_lane.append(other)

    nt = (((1,), (1,)), ((), ()))
    lane_o = lax.broadcasted_iota(I32, (tq, LANES), 1)
    for r0 in range(0, seq_q, tq):
        first_q = n_past + r0
        n_full = (first_q + 1) // KEY_ALIGN * KEY_ALIGN
        n_vis = min(-(-(first_q + tq) // KEY_ALIGN) * KEY_ALIGN, n_keys)
        k_pos = n_full + lax.broadcasted_iota(I32, (tq, n_vis - n_full), 1)
        q_pos = first_q + lax.broadcasted_iota(I32, (tq, n_vis - n_full), 0)
        visible = k_pos <= q_pos
        heads = []
        for hh in range(2):
            qa = q_aug[hh][r0:r0 + tq]
            s_edge = lax.dot_general(qa, k_aug[hh][n_full:n_vis], nt, preferred_element_type=F32)
            s_edge = jnp.where(visible, s_edge, -jnp.inf)
            m = jnp.max(s_edge, axis=1, keepdims=True)
            if n_full:
                s_full = lax.dot_general(qa, k_aug[hh][:n_full], nt, preferred_element_type=F32)
                m = jnp.maximum(m, jnp.max(s_full, axis=1, keepdims=True))
            acc = jnp.dot(jnp.exp2(s_edge - m).astype(BF16), v_aug[hh][n_full:n_vis],
                          preferred_element_type=F32)
            if n_full:
                acc = acc + jnp.dot(jnp.exp2(s_full - m).astype(BF16), v_aug[hh][:n_full],
                                    preferred_element_type=F32)
            denom = jnp.sum(jnp.where(lane_o == den_lane[hh], acc, 0.0), axis=1, keepdims=True)
            heads.append(acc / denom)
        o = jnp.where(lane_o < FOX_DH, heads[0], heads[1])
        o_ref[r0:r0 + tq, :] = (o * sg_ref[r0:r0 + tq, :].astype(F32)).astype(BF16)


def _fox_attention(q, sg, k_all, v_all, c_all, batch, seq_q, n_keys, n_past, tq):
    assert seq_q % tq == 0 and n_keys % KEY_ALIGN == 0
    kern = functools.partial(_fox_attn_kernel, seq_q=seq_q, tq=tq, n_past=n_past, n_keys=n_keys)
    return pl.pallas_call(
        kern,
        grid=(batch, FOX_PAIRS),
        in_specs=[pl.BlockSpec((seq_q, LANES), lambda b, hp: (b, hp)),
                  pl.BlockSpec((n_keys, LANES), lambda b, hp: (b, hp)),
                  pl.BlockSpec((n_keys, LANES), lambda b, hp: (b, hp)),
                  pl.BlockSpec((1, n_keys, LANES), lambda b, hp: (b, 0, 0)),
                  pl.BlockSpec((seq_q, LANES), lambda b, hp: (b, hp))],
        out_specs=pl.BlockSpec((seq_q, LANES), lambda b, hp: (b, hp)),
        out_shape=jax.ShapeDtypeStruct((batch * seq_q, D_MODEL), BF16),
        compiler_params=_cparams(("parallel", "parallel")),
        name="fox_attention",
    )(q, k_all, v_all, c_all, sg)


def _pad_cols(w, n):
    return jnp.pad(w, ((0, 0), (0, n - w.shape[1])))


def _prepare(norm_mix, norm_ffn, w_gla_in, w_gla_gate2, b_gla_gate, g_gla_head, w_gla_out, g_kv, w_kv,
             b_forget, g_k, w_fox_qg, g_q, w_fox_out, w_group, b_group, w_router, b_router,
             w_exp_gate, w_exp_up, w_exp_down):
    row = lambda a: a.reshape(1, -1).astype(F32)
    w_in = w_gla_in[0]
    c0, c1, c2, c3 = 2 * GLA_QK, 2 * GLA_QK + GLA_V, 2 * GLA_QK + GLA_V + GLA_GATE_RANK, w_in.shape[1]
    gla = dict(
        g=row(norm_mix[0]),
        w_qk=w_in[:, :c0].astype(BF16), w_v=w_in[:, c0:c1].astype(BF16), w_r=w_in[:, c2:c3].astype(BF16),
        w_g=_pad_cols(w_in[:, c1:c2], LANES).astype(BF16),
        w_g2=jnp.pad(w_gla_gate2[0], ((0, LANES - GLA_GATE_RANK), (0, 0))).astype(BF16),
        b_g=row(b_gla_gate[0]), g_head=row(g_gla_head[0]), w_out=w_gla_out[0].astype(BF16),
    )
    head_id = jnp.arange(D_MODEL) // FOX_DH
    fox = dict(
        g_kv=row(g_kv), g_mix=row(norm_mix[1]),
        w_k=w_kv[:, :D_MODEL].astype(BF16), w_v=w_kv[:, D_MODEL:2 * D_MODEL].astype(BF16),
        w_f=_pad_cols(w_kv[:, 2 * D_MODEL:], LANES).astype(BF16),
        b_f=_pad_cols(row(b_forget), LANES),
        w_q=w_fox_qg[0][:, :D_MODEL].astype(BF16), w_g=w_fox_qg[0][:, D_MODEL:].astype(BF16),
        g_k=row(jnp.tile(g_k, FOX_HEADS)), g_q=row(jnp.tile(g_q[0], FOX_HEADS)) * (FOX_DH ** -0.5 * LOG2E),
        head_sum=(head_id[:, None] == head_id[None, :]).astype(BF16),
        w_out=w_fox_out[0].astype(BF16),
    )
    moe = []
    for layer in range(2):
        w_r = _pad_cols(jnp.concatenate([w_group[layer], w_router[layer]], axis=1), LANES)
        w_r_hi = w_r.astype(BF16)
        moe.append(dict(
            g=row(norm_ffn[layer]),
            w_r_hi=w_r_hi, w_r_lo=(w_r - w_r_hi.astype(F32)).astype(BF16),
            b_r=_pad_cols(jnp.concatenate([row(b_group[layer]), row(b_router[layer])], axis=1), LANES),
            w_gu=jnp.concatenate([w_exp_gate[layer], w_exp_up[layer]], axis=2).astype(BF16),
            w_down=w_exp_down[layer].astype(BF16),
        ))
    return gla, fox, moe


def _trunk(x, s0, past, gla, fox, moe):
    batch, seq, _ = x.shape
    t = batch * seq
    h = x.reshape(t, D_MODEL)

    q, k, v, r, la = _gla_in_proj(h, gla["g"], gla["w_qk"], gla["w_v"], gla["w_r"], gla["w_g"], gla["w_g2"],
                                  gla["b_g"])
    o, s_new = _gla(q, k, v, la, s0, gla["g_head"], batch, seq)
    h = _out_proj(o, r, gla["w_out"], h)
    h = _moe(h, moe[0])

    k_new, v_new, lf_new, qf, sg = _fox_in_proj(h, fox)
    lf3 = lf_new.reshape(batch, seq, LANES)
    if past is None:
        n_past, n_keys = 0, seq
        tq = min(seq, 256)
        k_all, v_all, lf_all = k_new, v_new, lf3
    else:
        past_k, past_v, past_lf = past
        n_past = past_k.shape[1]
        tq = seq
        n_keys = -(-(n_past + seq) // KEY_ALIGN) * KEY_ALIGN
        pad = n_keys - n_past - seq
        cat = lambda a, b: jnp.concatenate(
            [a, b, jnp.zeros((batch, pad, a.shape[2]), F32)], axis=1)
        k_all = cat(past_k.reshape(batch, n_past, D_MODEL), k_new.reshape(batch, seq, D_MODEL)
                    ).reshape(batch * n_keys, D_MODEL)
        v_all = cat(past_v.reshape(batch, n_past, D_MODEL), v_new.reshape(batch, seq, D_MODEL)
                    ).reshape(batch * n_keys, D_MODEL)
        lf_all = cat(jnp.pad(past_lf, ((0, 0), (0, 0), (0, LANES - FOX_HEADS))), lf3)
    c_all = _cumsum(lf_all)
    o = _fox_attention(qf, sg, k_all, v_all, c_all, batch, seq, n_keys, n_past, tq)
    h = _out_proj(o, None, fox["w_out"], h)
    h = _moe(h, moe[1])

    return (h.reshape(batch, seq, D_MODEL),
            k_new.reshape(batch, seq, FOX_HEADS, FOX_DH),
            v_new.reshape(batch, seq, FOX_HEADS, FOX_DH),
            lf3[:, :, :FOX_HEADS],
            s_new[None])


def kernel(x_prompt, x_sample, cache_k, cache_v, cache_logf, state_gla, norm_mix, norm_ffn, w_gla_in, w_gla_gate2, b_gla_gate, g_gla_head, w_gla_out, g_kv, w_kv, b_forget, g_k, w_fox_qg, g_q, w_fox_out, w_group, b_group, w_router, b_router, w_exp_gate, w_exp_up, w_exp_down):
    gla, fox, moe = _prepare(norm_mix, norm_ffn, w_gla_in, w_gla_gate2, b_gla_gate, g_gla_head, w_gla_out, g_kv,
                             w_kv, b_forget, g_k, w_fox_qg, g_q, w_fox_out, w_group, b_group, w_router, b_router,
                             w_exp_gate, w_exp_up, w_exp_down)
    s_zero = jnp.zeros((x_prompt.shape[0], GLA_HEADS, GLA_DK, GLA_DV), F32)
    y_p, k_p, v_p, lf_p, s_p = _trunk(x_prompt, s_zero, None, gla, fox, moe)
    y_s, k_s, v_s, lf_s, s_s = _trunk(x_sample, state_gla[0].astype(F32),
                                      (cache_k.astype(F32), cache_v.astype(F32), cache_logf.astype(F32)),
                                      gla, fox, moe)
    return (y_p, y_s, k_p, v_p, lf_p, s_p, k_s, v_s, lf_s, s_s)
```

```python
import functools

import jax
import jax.numpy as jnp
from jax import lax
from jax.experimental import pallas as pl
from jax.experimental.pallas import tpu as pltpu

F32 = jnp.float32
BF16 = jnp.bfloat16
U32 = jnp.uint32
I32 = jnp.int32

EPS = 1e-6
D_MODEL = 1024
HALF_D = D_MODEL // 2
LANES = 128
VMEM_LIMIT_BYTES = 56 * 1024 * 1024

GLA_HEADS = 4
GLA_DK = 128
GLA_DV = 256
GLA_QK = GLA_HEADS * GLA_DK
GLA_V = GLA_HEADS * GLA_DV
GLA_GATE_RANK = 16
GLA_GATE_TAU = 16.0
GLA_CHUNK = 64
GLA_GROUP = 256

FOX_HEADS = 16
FOX_DH = 64
FOX_PAIRS = FOX_HEADS // 2

N_GROUPS = 4
EXPERTS_PER_GROUP = 8
N_EXPERTS = N_GROUPS * EXPERTS_PER_GROUP
D_EXPERT = 512
MOE_BLOCK = 256
ROW_DMA_UNROLL = 16


def _cparams(sem):
    return pltpu.CompilerParams(dimension_semantics=sem, vmem_limit_bytes=VMEM_LIMIT_BYTES)


def _row_tile(t, pref):
    tm = min(t, pref)
    assert t % tm == 0
    return tm


def _rms_scale(x):
    return lax.rsqrt(jnp.mean(x * x, axis=-1, keepdims=True) + EPS)


def _pack_halves(y):
    lo = lax.bitcast_convert_type(y[:, :HALF_D].astype(BF16).astype(F32), U32)
    hi = lax.bitcast_convert_type(y[:, HALF_D:].astype(BF16).astype(F32), U32)
    return (lo >> 16) | (hi & jnp.uint32(0xFFFF0000))


def _unpack_halves(p):
    lo = lax.bitcast_convert_type(p << 16, F32)
    hi = lax.bitcast_convert_type(p & jnp.uint32(0xFFFF0000), F32)
    return lo, hi


def _gla_in_kernel(h_ref, g_ref, wqk_ref, wv_ref, wr_ref, wg_ref, wg2_ref, bg_ref,
                   q_ref, k_ref, v_ref, r_ref, la_ref):
    x = h_ref[...]
    hn = (x * _rms_scale(x) * g_ref[...]).astype(BF16)
    qk = jnp.dot(hn, wqk_ref[...], preferred_element_type=F32)
    q_ref[...] = (qk[:, :GLA_QK] * (GLA_DK ** -0.5)).astype(BF16)
    k_ref[...] = qk[:, GLA_QK:].astype(BF16)
    v_ref[...] = jnp.dot(hn, wv_ref[...], preferred_element_type=F32).astype(BF16)
    r_ref[...] = jnp.dot(hn, wr_ref[...], preferred_element_type=F32).astype(BF16)
    g_lr = jnp.dot(hn, wg_ref[...], preferred_element_type=F32)
    z = jnp.dot(g_lr.astype(BF16), wg2_ref[...], preferred_element_type=F32) + bg_ref[...]
    la_ref[...] = jax.nn.log_sigmoid(z) / GLA_GATE_TAU


def _gla_in_proj(h, g, wqk, wv, wr, wg, wg2, bg):
    t = h.shape[0]
    tm = _row_tile(t, 512)
    row = lambda n: pl.BlockSpec((tm, n), lambda i: (i, 0))
    full = lambda a: pl.BlockSpec(a.shape, lambda i: (0,) * a.ndim)
    return pl.pallas_call(
        _gla_in_kernel,
        grid=(t // tm,),
        in_specs=[row(D_MODEL), full(g), full(wqk), full(wv), full(wr), full(wg), full(wg2), full(bg)],
        out_specs=[row(GLA_QK), row(GLA_QK), row(GLA_V), row(GLA_V), row(GLA_QK)],
        out_shape=[jax.ShapeDtypeStruct((t, GLA_QK), BF16), jax.ShapeDtypeStruct((t, GLA_QK), BF16),
                   jax.ShapeDtypeStruct((t, GLA_V), BF16), jax.ShapeDtypeStruct((t, GLA_V), BF16),
                   jax.ShapeDtypeStruct((t, GLA_QK), F32)],
        compiler_params=_cparams(("parallel",)),
        name="gla_in_proj",
    )(h, g, wqk, wv, wr, wg, wg2, bg)


def _gla_kernel(q_ref, k_ref, v_ref, la_ref, s0_ref, gh_ref, o_ref, s_ref,
                qb_s, oi_s, u_s, dl_s, *, seq, chunk, group):
    n_groups = seq // group
    n_chunks = seq // chunk
    per_group = group // chunk

    shift = chunk.bit_length() - 1
    assert chunk == 1 << shift and group & (group - 1) == 0
    row = lax.broadcasted_iota(I32, (group, group), 0)
    col = lax.broadcasted_iota(I32, (group, group), 1)
    tril = ((row >> shift) == (col >> shift)) & (col <= row)
    tril_bf = jnp.where(tril, 1.0, 0.0).astype(BF16)

    def chunk_rows(b, at):
        return jnp.concatenate(
            [jnp.broadcast_to(b[j * chunk + at:j * chunk + at + 1], (chunk, GLA_DK)) for j in range(per_group)],
            axis=0)

    def group_body(g, carry):
        r0 = pl.multiple_of(g * group, group)
        rows = pl.ds(r0, group)
        la = la_ref[rows, :]
        la_hi = la.astype(BF16)
        la_lo = (la - la_hi.astype(F32)).astype(BF16)
        b = (jnp.dot(tril_bf, la_hi, preferred_element_type=F32)
             + jnp.dot(tril_bf, la_lo, preferred_element_type=F32))
        b_ref = chunk_rows(b, chunk // 2)
        b_last = chunk_rows(b, chunk - 1)
        qf = q_ref[rows, :].astype(F32)
        kf = k_ref[rows, :].astype(F32)
        vg = v_ref[rows, :]
        qe = (qf * jnp.exp(b - b_ref)).astype(BF16)
        ke = (kf * jnp.exp(b_ref - b)).astype(BF16)
        att = lax.dot_general(qe, ke, (((1,), (1,)), ((), ())), preferred_element_type=F32)
        att = jnp.where(tril, att, 0.0).astype(BF16)
        oi_s[rows, :] = jnp.dot(att, vg, preferred_element_type=F32)
        qb_s[rows, :] = (qf * jnp.exp(b)).astype(BF16)
        kl = (kf * jnp.exp(b_last - b)).astype(BF16)
        decay = jnp.exp(b_last)
        for j in range(per_group):
            c = g * per_group + j
            sl = slice(j * chunk, (j + 1) * chunk)
            u_s[c] = lax.dot_general(vg[sl], kl[sl], (((0,), (0,)), ((), ())), preferred_element_type=F32)
            dl_s[c] = decay[j * chunk:j * chunk + 1]
        return carry

    lax.fori_loop(0, n_groups, group_body, 0, unroll=min(2, n_groups))

    gh = gh_ref[...]

    def chunk_body(c, st):
        r0 = pl.multiple_of(c * chunk, chunk)
        rows = pl.ds(r0, chunk)
        o = oi_s[rows, :] + lax.dot_general(qb_s[rows, :], st.astype(BF16), (((1,), (1,)), ((), ())),
                                            preferred_element_type=F32)
        o_ref[rows, :] = (o * _rms_scale(o) * gh).astype(BF16)
        return st * dl_s[c] + u_s[c]

    st = lax.fori_loop(0, n_chunks, chunk_body, s0_ref[0, 0].T, unroll=min(4, n_chunks))
    s_ref[0, 0] = st.T


def _gla(q, k, v, la, s0, g_head, batch, seq):
    chunk = min(seq, GLA_CHUNK)
    group = min(seq, GLA_GROUP)
    assert seq % group == 0 and group % chunk == 0
    n_chunks = seq // chunk
    kern = functools.partial(_gla_kernel, seq=seq, chunk=chunk, group=group)
    return pl.pallas_call(
        kern,
        grid=(batch, GLA_HEADS),
        in_specs=[pl.BlockSpec((seq, GLA_DK), lambda b, h: (b, h)),
                  pl.BlockSpec((seq, GLA_DK), lambda b, h: (b, h)),
                  pl.BlockSpec((seq, GLA_DV), lambda b, h: (b, h)),
                  pl.BlockSpec((seq, GLA_DK), lambda b, h: (b, h)),
                  pl.BlockSpec((1, 1, GLA_DK, GLA_DV), lambda b, h: (b, h, 0, 0)),
                  pl.BlockSpec((1, GLA_DV), lambda b, h: (0, 0))],
        out_specs=[pl.BlockSpec((seq, GLA_DV), lambda b, h: (b, h)),
                   pl.BlockSpec((1, 1, GLA_DK, GLA_DV), lambda b, h: (b, h, 0, 0))],
        out_shape=[jax.ShapeDtypeStruct((batch * seq, GLA_V), BF16),
                   jax.ShapeDtypeStruct((batch, GLA_HEADS, GLA_DK, GLA_DV), F32)],
        scratch_shapes=[pltpu.VMEM((seq, GLA_DK), BF16),
                        pltpu.VMEM((seq, GLA_DV), F32),
                        pltpu.VMEM((n_chunks, GLA_DV, GLA_DK), F32),
                        pltpu.VMEM((n_chunks, 1, GLA_DK), F32)],
        compiler_params=_cparams(("parallel", "parallel")),
        name="gla_scan",
    )(q, k, v, la, s0, g_head)


def _out_proj_kernel(*refs, gated):
    if gated:
        a_ref, r_ref, w_ref, h_ref, o_ref = refs
        r = r_ref[...].astype(F32)
        a = (a_ref[...].astype(F32) * (r * jax.nn.sigmoid(r))).astype(BF16)
    else:
        a_ref, w_ref, h_ref, o_ref = refs
        a = a_ref[...]
    o_ref[...] = h_ref[...] + jnp.dot(a, w_ref[...], preferred_element_type=F32)


def _out_proj(a, r, w, h):
    t = h.shape[0]
    tm = _row_tile(t, 512)
    row = pl.BlockSpec((tm, D_MODEL), lambda i: (i, 0))
    wspec = pl.BlockSpec(w.shape, lambda i: (0, 0))
    gated = r is not None
    args = (a, r, w, h) if gated else (a, w, h)
    specs = [row, row, wspec, row] if gated else [row, wspec, row]
    return pl.pallas_call(
        functools.partial(_out_proj_kernel, gated=gated),
        grid=(t // tm,),
        in_specs=specs,
        out_specs=row,
        out_shape=jax.ShapeDtypeStruct((t, D_MODEL), F32),
        compiler_params=_cparams(("parallel",)),
        name="out_proj_gated" if gated else "out_proj",
    )(*args)


REC_EA, REC_EB, REC_PAIR, REC_RANK_A, REC_RANK_B = range(5)
REC_WIDTH = 8
META_WA, META_WB, META_EA, META_EB = range(4)
ROW_WIDTH = HALF_D + LANES
PAIRS_PER_GROUP = EXPERTS_PER_GROUP * (EXPERTS_PER_GROUP - 1) // 2
N_PAIRS = N_GROUPS * PAIRS_PER_GROUP


def _router_kernel(h_ref, g_ref, whi_ref, wlo_ref, bias_ref, xp_ref, rec_t_ref, cnt_ref, carry_s, *, tm, by_pair):
    @pl.when(pl.program_id(0) == 0)
    def _():
        carry_s[...] = jnp.zeros_like(carry_s)

    x = h_ref[...]
    hn = x * _rms_scale(x) * g_ref[...]
    xp_ref[:, :HALF_D] = _pack_halves(hn)

    hi = hn.astype(BF16)
    lo = (hn - hi.astype(F32)).astype(BF16)
    w_hi = whi_ref[...]
    lg = (jnp.dot(hi, w_hi, preferred_element_type=F32) + jnp.dot(lo, w_hi, preferred_element_type=F32)
          + jnp.dot(hi, wlo_ref[...], preferred_element_type=F32) + bias_ref[...])

    lane = lax.broadcasted_iota(I32, (tm, LANES), 1).astype(F32)
    neg = jnp.float32(-jnp.inf)

    def masked_softmax(mask):
        m = jnp.max(jnp.where(mask, lg, neg), axis=1, keepdims=True)
        e = jnp.where(mask, jnp.exp(lg - m), 0.0)
        return e / jnp.sum(e, axis=1, keepdims=True)

    def top1(p, mask):
        v = jnp.max(jnp.where(mask, p, -1.0), axis=1, keepdims=True)
        idx = jnp.min(jnp.where(mask & (p == v), lane, float(LANES)), axis=1, keepdims=True)
        return v, idx

    gmask = lane < N_GROUPS
    g_top, g_idx = top1(masked_softmax(gmask), gmask)
    e_lo = N_GROUPS + EXPERTS_PER_GROUP * g_idx
    emask = (lane >= e_lo) & (lane < e_lo + EXPERTS_PER_GROUP)
    ep = masked_softmax(emask)
    p1, i1 = top1(ep, emask)
    mask2 = emask & (lane != i1)
    p2, i2 = top1(ep, mask2)
    denom = p1 + p2
    w0 = g_top * p1 / denom
    w1 = g_top * p2 / denom

    a0 = i1 - e_lo
    a1 = i2 - e_lo
    first_low = a0 < a1
    lo_l = jnp.where(first_low, a0, a1)
    hi_l = jnp.where(first_low, a1, a0)
    w_a = jnp.where(first_low, w0, w1)
    w_b = jnp.where(first_low, w1, w0)
    e_a = EXPERTS_PER_GROUP * g_idx + lo_l
    e_b = EXPERTS_PER_GROUP * g_idx + hi_l
    pair = (PAIRS_PER_GROUP * g_idx + lo_l * (2 * EXPERTS_PER_GROUP - 1 - lo_l) * 0.5 + (hi_l - lo_l - 1.0))

    meta = jnp.zeros((tm, LANES), F32)
    for slot, val in ((META_WA, w_a), (META_WB, w_b), (META_EA, e_a), (META_EB, e_b)):
        meta = jnp.where(lane == slot, val, meta)
    xp_ref[:, HALF_D:] = lax.bitcast_convert_type(meta, U32)

    if by_pair:
        hit_a = lane == pair
        onehot = jnp.where(hit_a, 1.0, 0.0)
    else:
        hit_a = lane == e_a
        hit_b = lane == e_b
        onehot = jnp.where(hit_a | hit_b, 1.0, 0.0)
    r_i = lax.broadcasted_iota(I32, (tm, tm), 0)
    c_i = lax.broadcasted_iota(I32, (tm, tm), 1)
    before = jnp.where(c_i < r_i, 1.0, 0.0).astype(BF16)
    prior = jnp.dot(before, onehot.astype(BF16), preferred_element_type=F32) + carry_s[...]
    rank_a = jnp.sum(jnp.where(hit_a, prior, 0.0), axis=1, keepdims=True)
    rank_b = jnp.zeros_like(rank_a) if by_pair else jnp.sum(jnp.where(hit_b, prior, 0.0), axis=1, keepdims=True)
    carry_s[...] = carry_s[...] + jnp.sum(onehot, axis=0, keepdims=True)
    cnt_ref[...] = carry_s[...]

    rec = jnp.zeros((tm, LANES), F32)
    for slot, val in ((REC_EA, e_a), (REC_EB, e_b), (REC_PAIR, pair), (REC_RANK_A, rank_a), (REC_RANK_B, rank_b)):
        rec = jnp.where(lane == slot, val, rec)
    rec_t_ref[...] = rec.T[:REC_WIDTH]


def _router(h, g, w_hi, w_lo, bias, by_pair):
    t = h.shape[0]
    tm = _row_tile(t, 512)
    full = lambda a: pl.BlockSpec(a.shape, lambda i: (0,) * a.ndim)
    return pl.pallas_call(
        functools.partial(_router_kernel, tm=tm, by_pair=by_pair),
        grid=(t // tm,),
        in_specs=[pl.BlockSpec((tm, D_MODEL), lambda i: (i, 0)), full(g), full(w_hi), full(w_lo), full(bias)],
        out_specs=[pl.BlockSpec((tm, ROW_WIDTH), lambda i: (i, 0)),
                   pl.BlockSpec((REC_WIDTH, tm), lambda i: (0, i)),
                   pl.BlockSpec((1, LANES), lambda i: (0, 0))],
        out_shape=[jax.ShapeDtypeStruct((t, ROW_WIDTH), U32),
                   jax.ShapeDtypeStruct((REC_WIDTH, t), F32),
                   jax.ShapeDtypeStruct((1, LANES), F32)],
        scratch_shapes=[pltpu.VMEM((1, LANES), F32)],
        compiler_params=_cparams(("arbitrary",)),
        name="moe_router",
    )(h, g, w_hi, w_lo, bias)


def _row_copy(src, src_row, dst, dst_row, sem):
    return pltpu.make_async_copy(src.at[pl.ds(src_row, 1)], dst.at[pl.ds(dst_row, 1)], sem)


def _for_each_row(tm, n_pos, fn):
    rows_per_trip = ROW_DMA_UNROLL // n_pos

    def trip(i, carry):
        for j in range(rows_per_trip):
            for k in range(n_pos):
                fn(i * rows_per_trip + j, k, j * n_pos + k)
        return carry

    lax.fori_loop(0, tm // rows_per_trip, trip, 0)


def _dispatch_kernel(pos_ref, xp_ref, zeros_hbm, xs_hbm, sem, *, tm, n_pos):
    del zeros_hbm
    _for_each_row(tm, n_pos, lambda r, k, n: _row_copy(
        xp_ref, r, xs_hbm, pos_ref[0, 0, k * tm + r], sem.at[0]).start(priority=n % 2))
    _for_each_row(tm, n_pos, lambda r, k, n: _row_copy(xp_ref, 0, xs_hbm, 0, sem.at[0]).wait())


def _dispatch(pos, xp, n_slots):
    t = xp.shape[0]
    tm = _row_tile(t, 256)
    n_pos = pos.shape[2] // tm
    return pl.pallas_call(
        functools.partial(_dispatch_kernel, tm=tm, n_pos=n_pos),
        grid=(t // tm,),
        in_specs=[pl.BlockSpec((1, 1, n_pos * tm), lambda i: (i, 0, 0), memory_space=pltpu.SMEM),
                  pl.BlockSpec((tm, ROW_WIDTH), lambda i: (i, 0)),
                  pl.BlockSpec(memory_space=pl.ANY)],
        out_specs=pl.BlockSpec(memory_space=pl.ANY),
        out_shape=jax.ShapeDtypeStruct((n_slots, ROW_WIDTH), U32),
        input_output_aliases={2: 0},
        scratch_shapes=[pltpu.SemaphoreType.DMA((1,))],
        compiler_params=_cparams(("arbitrary",)),
        name="moe_dispatch",
    )(pos, xp, jnp.zeros((n_slots, ROW_WIDTH), U32))


def _expert_kernel(*refs, n_exp, n_blocks):
    be_ref, nb_ref, xs_ref = refs[:3]
    w_refs = refs[3:3 + 3 * n_exp]
    out_ref = refs[3 + 3 * n_exp]
    scratch = refs[4 + 3 * n_exp:]
    b = pl.program_id(0)

    @pl.when(b < nb_ref[0])
    def _():
        prev = jnp.maximum(b - 1, 0)
        for j in range(n_exp):
            wg_ref, wu_ref, wd_ref = w_refs[3 * j:3 * j + 3]
            wgu_s, wd_s = scratch[2 * j:2 * j + 2]

            @pl.when((b == 0) | (be_ref[j * n_blocks + b] != be_ref[j * n_blocks + prev]))
            def _():
                wgu_s[:, :D_EXPERT] = wg_ref[0, 0].astype(BF16)
                wgu_s[:, D_EXPERT:] = wu_ref[0, 0].astype(BF16)
                wd_s[...] = wd_ref[0, 0].astype(BF16)

        rows = xs_ref[...]
        lo, hi = _unpack_halves(rows[:, :HALF_D])
        x_lo = lo.astype(BF16)
        x_hi = hi.astype(BF16)
        meta = lax.bitcast_convert_type(rows[:, HALF_D:], F32)
        y = None
        for j in range(n_exp):
            wgu_s, wd_s = scratch[2 * j:2 * j + 2]
            gu = (jnp.dot(x_lo, wgu_s[:HALF_D, :], preferred_element_type=F32)
                  + jnp.dot(x_hi, wgu_s[HALF_D:, :], preferred_element_type=F32))
            gate = gu[:, :D_EXPERT]
            if n_exp == 2:
                w = meta[:, META_WA + j:META_WA + j + 1]
            else:
                this = be_ref[b].astype(F32)
                w = jnp.where(meta[:, META_EA:META_EA + 1] == this,
                              meta[:, META_WA:META_WA + 1], meta[:, META_WB:META_WB + 1])
            hid = (w * (gate * jax.nn.sigmoid(gate) * gu[:, D_EXPERT:])).astype(BF16)
            y_j = jnp.dot(hid, wd_s[...], preferred_element_type=F32)
            y = y_j if y is None else y + y_j
        out_ref[...] = _pack_halves(y)

    @pl.when(b >= nb_ref[0])
    def _():
        out_ref[...] = jnp.zeros_like(out_ref)


def _experts(block_experts, n_blocks_used, xs, w_gate, w_up, w_down, layer, n_exp):
    n_slots = xs.shape[0]
    n_blocks = n_slots // MOE_BLOCK

    def used(b, nb):
        return jnp.minimum(b, nb[0] - 1)

    def w_spec(shape, j):
        return pl.BlockSpec((1, 1) + shape, lambda b, be, nb: (layer, be[j * n_blocks + used(b, nb)], 0, 0))

    w_specs, w_args, scratch = [], [], []
    for j in range(n_exp):
        w_specs += [w_spec((D_MODEL, D_EXPERT), j), w_spec((D_MODEL, D_EXPERT), j), w_spec((D_EXPERT, D_MODEL), j)]
        w_args += [w_gate, w_up, w_down]
        scratch += [pltpu.VMEM((D_MODEL, 2 * D_EXPERT), BF16), pltpu.VMEM((D_EXPERT, D_MODEL), BF16)]
    grid_spec = pltpu.PrefetchScalarGridSpec(
        num_scalar_prefetch=2,
        grid=(n_blocks,),
        in_specs=[pl.BlockSpec((MOE_BLOCK, ROW_WIDTH), lambda b, be, nb: (used(b, nb), 0))] + w_specs,
        out_specs=pl.BlockSpec((MOE_BLOCK, HALF_D), lambda b, be, nb: (b, 0)),
        scratch_shapes=scratch,
    )
    return pl.pallas_call(
        functools.partial(_expert_kernel, n_exp=n_exp, n_blocks=n_blocks),
        grid_spec=grid_spec,
        out_shape=jax.ShapeDtypeStruct((n_slots, HALF_D), U32),
        compiler_params=_cparams(("arbitrary",)),
        name="moe_experts",
    )(block_experts, n_blocks_used, xs, *w_args)


def _combine_kernel(*refs, tm, n_pos):
    pos_ref, h_ref, out_hbm, y_ref = refs[:4]
    gathered = refs[4:4 + n_pos]
    sem = refs[4 + n_pos]
    _for_each_row(tm, n_pos, lambda r, k, n: _row_copy(
        out_hbm, pos_ref[0, 0, k * tm + r], gathered[k], r, sem.at[0]).start(priority=n % 2))
    _for_each_row(tm, n_pos, lambda r, k, n: _row_copy(out_hbm, 0, gathered[k], 0, sem.at[0]).wait())

    lo, hi = _unpack_halves(gathered[0][...])
    for g_s in gathered[1:]:
        lo_k, hi_k = _unpack_halves(g_s[...])
        lo, hi = lo + lo_k, hi + hi_k
    y_ref[:, :HALF_D] = h_ref[:, :HALF_D] + lo
    y_ref[:, HALF_D:] = h_ref[:, HALF_D:] + hi


def _combine(pos, h, out_slots):
    t = h.shape[0]
    tm = _row_tile(t, 256)
    n_pos = pos.shape[2] // tm
    return pl.pallas_call(
        functools.partial(_combine_kernel, tm=tm, n_pos=n_pos),
        grid=(t // tm,),
        in_specs=[pl.BlockSpec((1, 1, n_pos * tm), lambda i: (i, 0, 0), memory_space=pltpu.SMEM),
                  pl.BlockSpec((tm, D_MODEL), lambda i: (i, 0)),
                  pl.BlockSpec(memory_space=pl.ANY)],
        out_specs=pl.BlockSpec((tm, D_MODEL), lambda i: (i, 0)),
        out_shape=jax.ShapeDtypeStruct((t, D_MODEL), F32),
        scratch_shapes=[pltpu.VMEM((tm, HALF_D), U32)] * n_pos + [pltpu.SemaphoreType.DMA((1,))],
        compiler_params=_cparams(("arbitrary",)),
        name="moe_combine",
    )(pos, h, out_slots)


def _pair_tables():
    pairs = [(a, b) for a in range(EXPERTS_PER_GROUP) for b in range(a + 1, EXPERTS_PER_GROUP)]
    lo = [EXPERTS_PER_GROUP * g + a for g in range(N_GROUPS) for a, _ in pairs]
    hi = [EXPERTS_PER_GROUP * g + b for g in range(N_GROUPS) for _, b in pairs]
    return jnp.array(lo, I32), jnp.array(hi, I32)


def _bins_are_pairs(t):
    return t // MOE_BLOCK >= 2 * N_PAIRS


def _moe(h, p, w_gate, w_up, w_down, layer):
    t = h.shape[0]
    by_pair = _bins_are_pairs(t)
    n_bins, n_pos, n_exp = (N_PAIRS, 1, 2) if by_pair else (N_EXPERTS, 2, 1)
    xp, rec_t, cnt = _router(h, p["g"], p["w_r_hi"], p["w_r_lo"], p["b_r"], by_pair)

    counts = cnt[0, :n_bins].astype(I32)
    blocks_per_bin = (counts + MOE_BLOCK - 1) // MOE_BLOCK
    block_end = jnp.cumsum(blocks_per_bin)
    slot_base = (block_end - blocks_per_bin) * MOE_BLOCK
    n_blocks = n_pos * t // MOE_BLOCK + n_bins
    n_slots = n_blocks * MOE_BLOCK
    block_bin = jnp.minimum(
        jnp.sum(block_end[None, :] <= jnp.arange(n_blocks, dtype=I32)[:, None], axis=1), n_bins - 1).astype(I32)
    n_used = block_end[-1:].astype(I32)
    bin_ids = jnp.arange(n_bins, dtype=F32)[:, None]
    base_of = lambda bin_row: jnp.sum(jnp.where(bin_row[None, :] == bin_ids, slot_base[:, None], 0), axis=0)
    tm = _row_tile(t, 256)
    if by_pair:
        pair_lo, pair_hi = _pair_tables()
        block_experts = jnp.concatenate([pair_lo[block_bin], pair_hi[block_bin]])
        pos = (base_of(rec_t[REC_PAIR]) + rec_t[REC_RANK_A].astype(I32)).reshape(t // tm, 1, tm)
    else:
        block_experts = block_bin
        pos = jnp.concatenate(
            [(base_of(rec_t[REC_EA]) + rec_t[REC_RANK_A].astype(I32)).reshape(t // tm, 1, tm),
             (base_of(rec_t[REC_EB]) + rec_t[REC_RANK_B].astype(I32)).reshape(t // tm, 1, tm)], axis=2)

    xs = _dispatch(pos, xp, n_slots)
    out_slots = _experts(block_experts, n_used, xs, w_gate, w_up, w_down, layer, n_exp)
    return _combine(pos, h, out_slots)


def _fox_in_kernel(h_ref, gkv_ref, gmix_ref, wk_ref, wv_ref, wf_ref, bf_ref, wq_ref, wg_ref,
                   gk_ref, gq_ref, bd_ref, k_ref, v_ref, lf_ref, q_ref, sg_ref, *, tm):
    x = h_ref[...]
    xn = x * _rms_scale(x)
    x_kv = (xn * gkv_ref[...]).astype(BF16)
    x_q = (xn * gmix_ref[...]).astype(BF16)
    bd = bd_ref[...]

    def head_norm(y):
        ms = jnp.dot((y * y).astype(BF16), bd, preferred_element_type=F32) * (1.0 / FOX_DH)
        return y * lax.rsqrt(ms + EPS)

    k = jnp.dot(x_kv, wk_ref[...], preferred_element_type=F32)
    k_ref[...] = head_norm(k) * gk_ref[...]
    v_ref[...] = jnp.dot(x_kv, wv_ref[...], preferred_element_type=F32)
    fl = jnp.dot(x_kv, wf_ref[...], preferred_element_type=F32) + bf_ref[...]
    lane = lax.broadcasted_iota(I32, (tm, LANES), 1)
    lf_ref[...] = jnp.where(lane < FOX_HEADS, jax.nn.log_sigmoid(fl), 0.0)
    q = jnp.dot(x_q, wq_ref[...], preferred_element_type=F32)
    q_ref[...] = (head_norm(q) * gq_ref[...]).astype(BF16)
    gate = jnp.dot(x_q, wg_ref[...], preferred_element_type=F32)
    sg_ref[...] = jax.nn.sigmoid(gate).astype(BF16)


def _fox_in_proj(h, p):
    t = h.shape[0]
    tm = _row_tile(t, 256)
    row = lambda n: pl.BlockSpec((tm, n), lambda i: (i, 0))
    full = lambda a: pl.BlockSpec(a.shape, lambda i: (0,) * a.ndim)
    consts = (p["g_kv"], p["g_mix"], p["w_k"], p["w_v"], p["w_f"], p["b_f"], p["w_q"], p["w_g"],
              p["g_k"], p["g_q"], p["head_sum"])
    return pl.pallas_call(
        functools.partial(_fox_in_kernel, tm=tm),
        grid=(t // tm,),
        in_specs=[row(D_MODEL)] + [full(a) for a in consts],
        out_specs=[row(D_MODEL), row(D_MODEL), row(LANES), row(D_MODEL), row(D_MODEL)],
        out_shape=[jax.ShapeDtypeStruct((t, D_MODEL), F32), jax.ShapeDtypeStruct((t, D_MODEL), F32),
                   jax.ShapeDtypeStruct((t, LANES), F32), jax.ShapeDtypeStruct((t, D_MODEL), BF16),
                   jax.ShapeDtypeStruct((t, D_MODEL), BF16)],
        compiler_params=_cparams(("parallel",)),
        name="fox_in_proj",
    )(h, *consts)


CUMSUM_ROWS = 128


def _cumsum_kernel(x_ref, c_ref, *, seq):
    r_i = lax.broadcasted_iota(I32, (CUMSUM_ROWS, CUMSUM_ROWS), 0)
    c_i = lax.broadcasted_iota(I32, (CUMSUM_ROWS, CUMSUM_ROWS), 1)
    tri = jnp.where(c_i <= r_i, 1.0, 0.0).astype(BF16)

    def body(g, carry):
        rows = pl.ds(pl.multiple_of(g * CUMSUM_ROWS, CUMSUM_ROWS), CUMSUM_ROWS)
        x = x_ref[0, rows, :]
        x1 = x.astype(BF16)
        rem = x - x1.astype(F32)
        x2 = rem.astype(BF16)
        x3 = (rem - x2.astype(F32)).astype(BF16)
        c = (jnp.dot(tri, x1, preferred_element_type=F32) + jnp.dot(tri, x2, preferred_element_type=F32)
             + jnp.dot(tri, x3, preferred_element_type=F32)) + carry
        c_ref[0, rows, :] = c
        return c[CUMSUM_ROWS - 1:CUMSUM_ROWS, :]

    lax.fori_loop(0, seq // CUMSUM_ROWS, body, jnp.zeros((1, LANES), F32))


def _cumsum(x):
    batch, seq, _ = x.shape
    assert seq % CUMSUM_ROWS == 0
    spec = pl.BlockSpec((1, seq, LANES), lambda b: (b, 0, 0))
    return pl.pallas_call(
        functools.partial(_cumsum_kernel, seq=seq),
        grid=(batch,),
        in_specs=[spec],
        out_specs=spec,
        out_shape=jax.ShapeDtypeStruct(x.shape, F32),
        compiler_params=_cparams(("parallel",)),
        name="logf_cumsum",
    )(x)


LOG2E = 1.4426950408889634
KEY_ALIGN = 128


def _split3(x):
    x1 = x.astype(BF16).astype(F32)
    r = x - x1
    x2 = r.astype(BF16).astype(F32)
    return x1, x2, r - x2


def _fox_attn_kernel(q_ref, k_ref, v_ref, c_ref, sg_ref, o_ref, *, seq_q, tq, n_past, n_keys):
    hp = pl.program_id(1)
    lane_k = lax.broadcasted_iota(I32, (n_keys, LANES), 1)
    lane_q = lax.broadcasted_iota(I32, (seq_q, LANES), 1)
    c_pair = c_ref[0] * LOG2E
    k_pair = k_ref[...]
    v_pair = v_ref[...]
    q_pair = q_ref[...].astype(F32)

    k_aug, v_aug, q_aug, den_lane = [], [], [], []
    for hh in range(2):
        own_lo = hh * FOX_DH
        other = FOX_DH - own_lo
        c_k = jnp.sum(jnp.where(lane_k == 2 * hp + hh, c_pair, 0.0), axis=1, keepdims=True)
        c1, c2, c3 = _split3(c_k)
        d_k = lane_k - other
        extra_k = jnp.where((d_k >= 0) & (d_k < 3), 1.0,
                            jnp.where(d_k == 3, -c1, jnp.where(d_k == 4, -c2, jnp.where(d_k == 5, -c3, 0.0))))
        own_k = (lane_k >= own_lo) & (lane_k < own_lo + FOX_DH)
        k_aug.append(jnp.where(own_k, k_pair, extra_k).astype(BF16))
        v_aug.append(jnp.where(own_k, v_pair, jnp.where(d_k == 0, 1.0, 0.0)).astype(BF16))
        q1, q2, q3 = (c[n_past:n_past + seq_q] for c in (c1, c2, c3))
        d_q = lane_q - other
        extra_q = jnp.where(d_q == 0, q1, jnp.where(d_q == 1, q2, jnp.where(d_q == 2, q3,
                            jnp.where((d_q >= 3) & (d_q < 6), 1.0, 0.0))))
        own_q = (lane_q >= own_lo) & (lane_q < own_lo + FOX_DH)
        q_aug.append(jnp.where(own_q, q_pair, extra_q).astype(BF16))
        den_lane.append(other)

    nt = (((1,), (1,)), ((), ()))
    lane_o = lax.broadcasted_iota(I32, (tq, LANES), 1)
    for r0 in range(0, seq_q, tq):
        first_q = n_past + r0
        n_full = (first_q + 1) // KEY_ALIGN * KEY_ALIGN
        n_vis = min(-(-(first_q + tq) // KEY_ALIGN) * KEY_ALIGN, n_keys)
        k_pos = n_full + lax.broadcasted_iota(I32, (tq, n_vis - n_full), 1)
        q_pos = first_q + lax.broadcasted_iota(I32, (tq, n_vis - n_full), 0)
        visible = k_pos <= q_pos
        heads = []
        for hh in range(2):
            qa = q_aug[hh][r0:r0 + tq]
            s_edge = lax.dot_general(qa, k_aug[hh][n_full:n_vis], nt, preferred_element_type=F32)
            s_edge = jnp.where(visible, s_edge, -jnp.inf)
            m = jnp.max(s_edge, axis=1, keepdims=True)
            if n_full:
                s_full = lax.dot_general(qa, k_aug[hh][:n_full], nt, preferred_element_type=F32)
                m = jnp.maximum(m, jnp.max(s_full, axis=1, keepdims=True))
            acc = jnp.dot(jnp.exp2(s_edge - m).astype(BF16), v_aug[hh][n_full:n_vis],
                          preferred_element_type=F32)
            if n_full:
                acc = acc + jnp.dot(jnp.exp2(s_full - m).astype(BF16), v_aug[hh][:n_full],
                                    preferred_element_type=F32)
            denom = jnp.sum(jnp.where(lane_o == den_lane[hh], acc, 0.0), axis=1, keepdims=True)
            heads.append(acc / denom)
        o = jnp.where(lane_o < FOX_DH, heads[0], heads[1])
        o_ref[r0:r0 + tq, :] = (o * sg_ref[r0:r0 + tq, :].astype(F32)).astype(BF16)


def _fox_attention(q, sg, k_all, v_all, c_all, batch, seq_q, n_keys, n_past, tq):
    assert seq_q % tq == 0 and n_keys % KEY_ALIGN == 0
    kern = functools.partial(_fox_attn_kernel, seq_q=seq_q, tq=tq, n_past=n_past, n_keys=n_keys)
    return pl.pallas_call(
        kern,
        grid=(batch, FOX_PAIRS),
        in_specs=[pl.BlockSpec((seq_q, LANES), lambda b, hp: (b, hp)),
                  pl.BlockSpec((n_keys, LANES), lambda b, hp: (b, hp)),
                  pl.BlockSpec((n_keys, LANES), lambda b, hp: (b, hp)),
                  pl.BlockSpec((1, n_keys, LANES), lambda b, hp: (b, 0, 0)),
                  pl.BlockSpec((seq_q, LANES), lambda b, hp: (b, hp))],
        out_specs=pl.BlockSpec((seq_q, LANES), lambda b, hp: (b, hp)),
        out_shape=jax.ShapeDtypeStruct((batch * seq_q, D_MODEL), BF16),
        compiler_params=_cparams(("parallel", "parallel")),
        name="fox_attention",
    )(q, k_all, v_all, c_all, sg)


def _pad_cols(w, n):
    return jnp.pad(w, ((0, 0), (0, n - w.shape[1])))


def _prepare(norm_mix, norm_ffn, w_gla_in, w_gla_gate2, b_gla_gate, g_gla_head, w_gla_out, g_kv, w_kv,
             b_forget, g_k, w_fox_qg, g_q, w_fox_out, w_group, b_group, w_router, b_router):
    row = lambda a: a.reshape(1, -1).astype(F32)
    w_in = w_gla_in[0]
    c0, c1, c2, c3 = 2 * GLA_QK, 2 * GLA_QK + GLA_V, 2 * GLA_QK + GLA_V + GLA_GATE_RANK, w_in.shape[1]
    gla = dict(
        g=row(norm_mix[0]),
        w_qk=w_in[:, :c0].astype(BF16), w_v=w_in[:, c0:c1].astype(BF16), w_r=w_in[:, c2:c3].astype(BF16),
        w_g=_pad_cols(w_in[:, c1:c2], LANES).astype(BF16),
        w_g2=jnp.pad(w_gla_gate2[0], ((0, LANES - GLA_GATE_RANK), (0, 0))).astype(BF16),
        b_g=row(b_gla_gate[0]), g_head=row(g_gla_head[0]), w_out=w_gla_out[0].astype(BF16),
    )
    head_id = jnp.arange(D_MODEL) // FOX_DH
    fox = dict(
        g_kv=row(g_kv), g_mix=row(norm_mix[1]),
        w_k=w_kv[:, :D_MODEL].astype(BF16), w_v=w_kv[:, D_MODEL:2 * D_MODEL].astype(BF16),
        w_f=_pad_cols(w_kv[:, 2 * D_MODEL:], LANES).astype(BF16),
        b_f=_pad_cols(row(b_forget), LANES),
        w_q=w_fox_qg[0][:, :D_MODEL].astype(BF16), w_g=w_fox_qg[0][:, D_MODEL:].astype(BF16),
        g_k=row(jnp.tile(g_k, FOX_HEADS)), g_q=row(jnp.tile(g_q[0], FOX_HEADS)) * (FOX_DH ** -0.5 * LOG2E),
        head_sum=(head_id[:, None] == head_id[None, :]).astype(BF16),
        w_out=w_fox_out[0].astype(BF16),
    )
    moe = []
    for layer in range(2):
        w_r = _pad_cols(jnp.concatenate([w_group[layer], w_router[layer]], axis=1), LANES)
        w_r_hi = w_r.astype(BF16)
        moe.append(dict(
            g=row(norm_ffn[layer]),
            w_r_hi=w_r_hi, w_r_lo=(w_r - w_r_hi.astype(F32)).astype(BF16),
            b_r=_pad_cols(jnp.concatenate([row(b_group[layer]), row(b_router[layer])], axis=1), LANES),
        ))
    return gla, fox, moe


def _trunk(x, s0, past, gla, fox, moe, experts):
    batch, seq, _ = x.shape
    t = batch * seq
    h = x.reshape(t, D_MODEL)

    q, k, v, r, la = _gla_in_proj(h, gla["g"], gla["w_qk"], gla["w_v"], gla["w_r"], gla["w_g"], gla["w_g2"],
                                  gla["b_g"])
    o, s_new = _gla(q, k, v, la, s0, gla["g_head"], batch, seq)
    h = _out_proj(o, r, gla["w_out"], h)
    h = _moe(h, moe[0], *experts, 0)

    k_new, v_new, lf_new, qf, sg = _fox_in_proj(h, fox)
    lf3 = lf_new.reshape(batch, seq, LANES)
    if past is None:
        n_past, n_keys = 0, seq
        tq = min(seq, 256)
        k_all, v_all, lf_all = k_new, v_new, lf3
    else:
        past_k, past_v, past_lf = past
        n_past = past_k.shape[1]
        tq = seq
        n_keys = -(-(n_past + seq) // KEY_ALIGN) * KEY_ALIGN
        pad = n_keys - n_past - seq
        cat = lambda a, b: jnp.concatenate(
            [a, b, jnp.zeros((batch, pad, a.shape[2]), F32)], axis=1)
        k_all = cat(past_k.reshape(batch, n_past, D_MODEL), k_new.reshape(batch, seq, D_MODEL)
                    ).reshape(batch * n_keys, D_MODEL)
        v_all = cat(past_v.reshape(batch, n_past, D_MODEL), v_new.reshape(batch, seq, D_MODEL)
                    ).reshape(batch * n_keys, D_MODEL)
        lf_all = cat(jnp.pad(past_lf, ((0, 0), (0, 0), (0, LANES - FOX_HEADS))), lf3)
    c_all = _cumsum(lf_all)
    o = _fox_attention(qf, sg, k_all, v_all, c_all, batch, seq, n_keys, n_past, tq)
    h = _out_proj(o, None, fox["w_out"], h)
    h = _moe(h, moe[1], *experts, 1)

    return (h.reshape(batch, seq, D_MODEL),
            k_new.reshape(batch, seq, FOX_HEADS, FOX_DH),
            v_new.reshape(batch, seq, FOX_HEADS, FOX_DH),
            lf3[:, :, :FOX_HEADS],
            s_new[None])


def kernel(x_prompt, x_sample, cache_k, cache_v, cache_logf, state_gla, norm_mix, norm_ffn, w_gla_in, w_gla_gate2, b_gla_gate, g_gla_head, w_gla_out, g_kv, w_kv, b_forget, g_k, w_fox_qg, g_q, w_fox_out, w_group, b_group, w_router, b_router, w_exp_gate, w_exp_up, w_exp_down):
    gla, fox, moe = _prepare(norm_mix, norm_ffn, w_gla_in, w_gla_gate2, b_gla_gate, g_gla_head, w_gla_out, g_kv,
                             w_kv, b_forget, g_k, w_fox_qg, g_q, w_fox_out, w_group, b_group, w_router, b_router)
    experts = (w_exp_gate.astype(F32), w_exp_up.astype(F32), w_exp_down.astype(F32))
    s_zero = jnp.zeros((x_prompt.shape[0], GLA_HEADS, GLA_DK, GLA_DV), F32)
    y_p, k_p, v_p, lf_p, s_p = _trunk(x_prompt, s_zero, None, gla, fox, moe, experts)
    y_s, k_s, v_s, lf_s, s_s = _trunk(x_sample, state_gla[0].astype(F32),
                                      (cache_k.astype(F32), cache_v.astype(F32), cache_logf.astype(F32)),
                                      gla, fox, moe, experts)
    return (y_p, y_s, k_p, v_p, lf_p, s_p, k_s, v_s, lf_s, s_s)
```

```python
import functools

import jax
import jax.numpy as jnp
from jax import lax
from jax.experimental import pallas as pl
from jax.experimental.pallas import tpu as pltpu

F32 = jnp.float32
BF16 = jnp.bfloat16
U32 = jnp.uint32
I32 = jnp.int32

EPS = 1e-6
D_MODEL = 1024
HALF_D = D_MODEL // 2
LANES = 128
VMEM_LIMIT_BYTES = 56 * 1024 * 1024

GLA_HEADS = 4
GLA_DK = 128
GLA_DV = 256
GLA_QK = GLA_HEADS * GLA_DK
GLA_V = GLA_HEADS * GLA_DV
GLA_GATE_RANK = 16
GLA_GATE_TAU = 16.0
GLA_CHUNK = 64
GLA_GROUP = 256

FOX_HEADS = 16
FOX_DH = 64
FOX_PAIRS = FOX_HEADS // 2

N_GROUPS = 4
EXPERTS_PER_GROUP = 8
N_EXPERTS = N_GROUPS * EXPERTS_PER_GROUP
D_EXPERT = 512
MOE_BLOCK = 256
ROW_DMA_UNROLL = 16


def _cparams(sem):
    return pltpu.CompilerParams(dimension_semantics=sem, vmem_limit_bytes=VMEM_LIMIT_BYTES)


def _row_tile(t, pref):
    tm = min(t, pref)
    assert t % tm == 0
    return tm


def _resident(a):
    return pl.BlockSpec(a.shape, lambda *_: (0,) * a.ndim, pipeline_mode=pl.Buffered(1))


def _rms_scale(x):
    return lax.rsqrt(jnp.mean(x * x, axis=-1, keepdims=True) + EPS)


def _pack_halves(y):
    lo = lax.bitcast_convert_type(y[:, :HALF_D].astype(BF16).astype(F32), U32)
    hi = lax.bitcast_convert_type(y[:, HALF_D:].astype(BF16).astype(F32), U32)
    return (lo >> 16) | (hi & jnp.uint32(0xFFFF0000))


def _unpack_halves(p):
    lo = lax.bitcast_convert_type(p << 16, F32)
    hi = lax.bitcast_convert_type(p & jnp.uint32(0xFFFF0000), F32)
    return lo, hi


TILE_ROWS = 8
ROW_PIECES = D_MODEL // LANES
assert ROW_PIECES == TILE_ROWS


def _load_tiled_rows(ref, n, pieces):
    return jnp.concatenate([ref[pl.ds(j, n, stride=TILE_ROWS), :] for j in pieces], axis=1)


def _store_tiled_rows(ref, val):
    n = val.shape[0]
    for j in range(ROW_PIECES):
        ref[pl.ds(j, n, stride=TILE_ROWS), :] = val[:, j * LANES:(j + 1) * LANES]


def _gla_in_kernel(h_ref, g_ref, wqk_ref, wv_ref, wr_ref, wg_ref, wg2_ref, bg_ref,
                   q_ref, k_ref, v_ref, r_ref, la_ref):
    x = h_ref[...]
    hn = (x * _rms_scale(x) * g_ref[...]).astype(BF16)
    qk = jnp.dot(hn, wqk_ref[...], preferred_element_type=F32)
    q_ref[...] = (qk[:, :GLA_QK] * (GLA_DK ** -0.5)).astype(BF16)
    k_ref[...] = qk[:, GLA_QK:].astype(BF16)
    v_ref[...] = jnp.dot(hn, wv_ref[...], preferred_element_type=F32).astype(BF16)
    r_ref[...] = jnp.dot(hn, wr_ref[...], preferred_element_type=F32).astype(BF16)
    g_lr = jnp.dot(hn, wg_ref[...], preferred_element_type=F32)
    z = jnp.dot(g_lr.astype(BF16), wg2_ref[...], preferred_element_type=F32) + bg_ref[...]
    la_ref[...] = jax.nn.log_sigmoid(z) / GLA_GATE_TAU


def _gla_in_proj(h, g, wqk, wv, wr, wg, wg2, bg):
    t = h.shape[0]
    tm = _row_tile(t, 512)
    row = lambda n: pl.BlockSpec((tm, n), lambda i: (i, 0))
    full = _resident
    return pl.pallas_call(
        _gla_in_kernel,
        grid=(t // tm,),
        in_specs=[row(D_MODEL), full(g), full(wqk), full(wv), full(wr), full(wg), full(wg2), full(bg)],
        out_specs=[row(GLA_QK), row(GLA_QK), row(GLA_V), row(GLA_V), row(GLA_QK)],
        out_shape=[jax.ShapeDtypeStruct((t, GLA_QK), BF16), jax.ShapeDtypeStruct((t, GLA_QK), BF16),
                   jax.ShapeDtypeStruct((t, GLA_V), BF16), jax.ShapeDtypeStruct((t, GLA_V), BF16),
                   jax.ShapeDtypeStruct((t, GLA_QK), F32)],
        compiler_params=_cparams(("parallel",)),
        name="gla_in_proj",
    )(h, g, wqk, wv, wr, wg, wg2, bg)


def _gla_kernel(q_ref, k_ref, v_ref, la_ref, s0_ref, gh_ref, o_ref, s_ref,
                qb_s, oi_s, u_s, dl_s, *, seq, chunk, group):
    n_groups = seq // group
    n_chunks = seq // chunk
    per_group = group // chunk

    shift = chunk.bit_length() - 1
    assert chunk == 1 << shift and group & (group - 1) == 0
    row = lax.broadcasted_iota(I32, (group, group), 0)
    col = lax.broadcasted_iota(I32, (group, group), 1)
    tril = ((row >> shift) == (col >> shift)) & (col <= row)
    tril_bf = jnp.where(tril, 1.0, 0.0).astype(BF16)

    def chunk_rows(b, at):
        return jnp.concatenate(
            [jnp.broadcast_to(b[j * chunk + at:j * chunk + at + 1], (chunk, GLA_DK)) for j in range(per_group)],
            axis=0)

    def group_body(g, carry):
        r0 = pl.multiple_of(g * group, group)
        rows = pl.ds(r0, group)
        la = la_ref[rows, :]
        la_hi = la.astype(BF16)
        la_lo = (la - la_hi.astype(F32)).astype(BF16)
        b = (jnp.dot(tril_bf, la_hi, preferred_element_type=F32)
             + jnp.dot(tril_bf, la_lo, preferred_element_type=F32))
        b_ref = chunk_rows(b, chunk // 2)
        b_last = chunk_rows(b, chunk - 1)
        qf = q_ref[rows, :].astype(F32)
        kf = k_ref[rows, :].astype(F32)
        vg = v_ref[rows, :]
        qe = (qf * jnp.exp(b - b_ref)).astype(BF16)
        ke = (kf * jnp.exp(b_ref - b)).astype(BF16)
        att = lax.dot_general(qe, ke, (((1,), (1,)), ((), ())), preferred_element_type=F32)
        att = jnp.where(tril, att, 0.0).astype(BF16)
        oi_s[rows, :] = jnp.dot(att, vg, preferred_element_type=F32)
        qb_s[rows, :] = (qf * jnp.exp(b)).astype(BF16)
        kl = (kf * jnp.exp(b_last - b)).astype(BF16)
        decay = jnp.exp(b_last)
        for j in range(per_group):
            c = g * per_group + j
            sl = slice(j * chunk, (j + 1) * chunk)
            u_s[c] = lax.dot_general(vg[sl], kl[sl], (((0,), (0,)), ((), ())), preferred_element_type=F32)
            dl_s[c] = decay[j * chunk:j * chunk + 1]
        return carry

    lax.fori_loop(0, n_groups, group_body, 0, unroll=min(2, n_groups))

    gh = gh_ref[...]

    def chunk_body(c, st):
        r0 = pl.multiple_of(c * chunk, chunk)
        rows = pl.ds(r0, chunk)
        o = oi_s[rows, :] + lax.dot_general(qb_s[rows, :], st.astype(BF16), (((1,), (1,)), ((), ())),
                                            preferred_element_type=F32)
        o_ref[rows, :] = (o * _rms_scale(o) * gh).astype(BF16)
        return st * dl_s[c] + u_s[c]

    st = lax.fori_loop(0, n_chunks, chunk_body, s0_ref[0, 0].T, unroll=min(4, n_chunks))
    s_ref[0, 0] = st.T


def _gla(q, k, v, la, s0, g_head, batch, seq):
    chunk = min(seq, GLA_CHUNK)
    group = min(seq, GLA_GROUP)
    assert seq % group == 0 and group % chunk == 0
    n_chunks = seq // chunk
    kern = functools.partial(_gla_kernel, seq=seq, chunk=chunk, group=group)
    return pl.pallas_call(
        kern,
        grid=(batch, GLA_HEADS),
        in_specs=[pl.BlockSpec((seq, GLA_DK), lambda b, h: (b, h)),
                  pl.BlockSpec((seq, GLA_DK), lambda b, h: (b, h)),
                  pl.BlockSpec((seq, GLA_DV), lambda b, h: (b, h)),
                  pl.BlockSpec((seq, GLA_DK), lambda b, h: (b, h)),
                  pl.BlockSpec((1, 1, GLA_DK, GLA_DV), lambda b, h: (b, h, 0, 0)),
                  pl.BlockSpec((1, GLA_DV), lambda b, h: (0, 0))],
        out_specs=[pl.BlockSpec((seq, GLA_DV), lambda b, h: (b, h)),
                   pl.BlockSpec((1, 1, GLA_DK, GLA_DV), lambda b, h: (b, h, 0, 0))],
        out_shape=[jax.ShapeDtypeStruct((batch * seq, GLA_V), BF16),
                   jax.ShapeDtypeStruct((batch, GLA_HEADS, GLA_DK, GLA_DV), F32)],
        scratch_shapes=[pltpu.VMEM((seq, GLA_DK), BF16),
                        pltpu.VMEM((seq, GLA_DV), F32),
                        pltpu.VMEM((n_chunks, GLA_DV, GLA_DK), F32),
                        pltpu.VMEM((n_chunks, 1, GLA_DK), F32)],
        compiler_params=_cparams(("parallel", "parallel")),
        name="gla_scan",
    )(q, k, v, la, s0, g_head)


def _out_proj_kernel(*refs, gated):
    if gated:
        a_ref, r_ref, w_ref, h_ref, o_ref = refs
        r = r_ref[...].astype(F32)
        a = (a_ref[...].astype(F32) * (r * jax.nn.sigmoid(r))).astype(BF16)
    else:
        a_ref, w_ref, h_ref, o_ref = refs
        a = a_ref[...]
    o_ref[...] = h_ref[...] + jnp.dot(a, w_ref[...], preferred_element_type=F32)


def _out_proj(a, r, w, h):
    t = h.shape[0]
    tm = _row_tile(t, 512)
    row = pl.BlockSpec((tm, D_MODEL), lambda i: (i, 0))
    wspec = _resident(w)
    gated = r is not None
    args = (a, r, w, h) if gated else (a, w, h)
    specs = [row, row, wspec, row] if gated else [row, wspec, row]
    return pl.pallas_call(
        functools.partial(_out_proj_kernel, gated=gated),
        grid=(t // tm,),
        in_specs=specs,
        out_specs=row,
        out_shape=jax.ShapeDtypeStruct((t, D_MODEL), F32),
        compiler_params=_cparams(("parallel",)),
        name="out_proj_gated" if gated else "out_proj",
    )(*args)


REC_EA, REC_EB, REC_PAIR, REC_RANK_A, REC_RANK_B = range(5)
REC_WIDTH = 8
META_WA, META_WB, META_EA, META_EB = range(4)
META_PIECE = HALF_D // LANES
IN_PIECES = range(META_PIECE + 1)
PAIRS_PER_GROUP = EXPERTS_PER_GROUP * (EXPERTS_PER_GROUP - 1) // 2
N_PAIRS = N_GROUPS * PAIRS_PER_GROUP


def _router_kernel(h_ref, g_ref, whi_ref, wlo_ref, bias_ref, xp_ref, rec_t_ref, cnt_ref, carry_s, *, tm, by_pair):
    @pl.when(pl.program_id(0) == 0)
    def _():
        carry_s[...] = jnp.zeros_like(carry_s)

    x = h_ref[...]
    hn = x * _rms_scale(x) * g_ref[...]

    hi = hn.astype(BF16)
    lo = (hn - hi.astype(F32)).astype(BF16)
    w_hi = whi_ref[...]
    lg = (jnp.dot(hi, w_hi, preferred_element_type=F32) + jnp.dot(lo, w_hi, preferred_element_type=F32)
          + jnp.dot(hi, wlo_ref[...], preferred_element_type=F32) + bias_ref[...])

    lane = lax.broadcasted_iota(I32, (tm, LANES), 1).astype(F32)
    neg = jnp.float32(-jnp.inf)

    def masked_softmax(mask):
        m = jnp.max(jnp.where(mask, lg, neg), axis=1, keepdims=True)
        e = jnp.where(mask, jnp.exp(lg - m), 0.0)
        return e / jnp.sum(e, axis=1, keepdims=True)

    def top1(p, mask):
        v = jnp.max(jnp.where(mask, p, -1.0), axis=1, keepdims=True)
        idx = jnp.min(jnp.where(mask & (p == v), lane, float(LANES)), axis=1, keepdims=True)
        return v, idx

    gmask = lane < N_GROUPS
    g_top, g_idx = top1(masked_softmax(gmask), gmask)
    e_lo = N_GROUPS + EXPERTS_PER_GROUP * g_idx
    emask = (lane >= e_lo) & (lane < e_lo + EXPERTS_PER_GROUP)
    ep = masked_softmax(emask)
    p1, i1 = top1(ep, emask)
    mask2 = emask & (lane != i1)
    p2, i2 = top1(ep, mask2)
    denom = p1 + p2
    w0 = g_top * p1 / denom
    w1 = g_top * p2 / denom

    a0 = i1 - e_lo
    a1 = i2 - e_lo
    first_low = a0 < a1
    lo_l = jnp.where(first_low, a0, a1)
    hi_l = jnp.where(first_low, a1, a0)
    w_a = jnp.where(first_low, w0, w1)
    w_b = jnp.where(first_low, w1, w0)
    e_a = EXPERTS_PER_GROUP * g_idx + lo_l
    e_b = EXPERTS_PER_GROUP * g_idx + hi_l
    pair = (PAIRS_PER_GROUP * g_idx + lo_l * (2 * EXPERTS_PER_GROUP - 1 - lo_l) * 0.5 + (hi_l - lo_l - 1.0))

    meta = jnp.zeros((tm, LANES), F32)
    for slot, val in ((META_WA, w_a), (META_WB, w_b), (META_EA, e_a), (META_EB, e_b)):
        meta = jnp.where(lane == slot, val, meta)
    _store_tiled_rows(xp_ref, jnp.concatenate(
        [_pack_halves(hn), lax.bitcast_convert_type(meta, U32),
         jnp.zeros((tm, D_MODEL - HALF_D - LANES), U32)], axis=1))

    if by_pair:
        hit_a = lane == pair
        onehot = jnp.where(hit_a, 1.0, 0.0)
    else:
        hit_a = lane == e_a
        hit_b = lane == e_b
        onehot = jnp.where(hit_a | hit_b, 1.0, 0.0)
    r_i = lax.broadcasted_iota(I32, (tm, tm), 0)
    c_i = lax.broadcasted_iota(I32, (tm, tm), 1)
    before = jnp.where(c_i < r_i, 1.0, 0.0).astype(BF16)
    prior = jnp.dot(before, onehot.astype(BF16), preferred_element_type=F32) + carry_s[...]
    rank_a = jnp.sum(jnp.where(hit_a, prior, 0.0), axis=1, keepdims=True)
    rank_b = jnp.zeros_like(rank_a) if by_pair else jnp.sum(jnp.where(hit_b, prior, 0.0), axis=1, keepdims=True)
    carry_s[...] = carry_s[...] + jnp.sum(onehot, axis=0, keepdims=True)
    cnt_ref[...] = carry_s[...]

    rec = jnp.zeros((tm, LANES), F32)
    for slot, val in ((REC_EA, e_a), (REC_EB, e_b), (REC_PAIR, pair), (REC_RANK_A, rank_a), (REC_RANK_B, rank_b)):
        rec = jnp.where(lane == slot, val, rec)
    rec_t_ref[...] = rec.T[:REC_WIDTH]


def _router(h, g, w_hi, w_lo, bias, by_pair):
    t = h.shape[0]
    tm = _row_tile(t, 512)
    full = _resident
    return pl.pallas_call(
        functools.partial(_router_kernel, tm=tm, by_pair=by_pair),
        grid=(t // tm,),
        in_specs=[pl.BlockSpec((tm, D_MODEL), lambda i: (i, 0)), full(g), full(w_hi), full(w_lo), full(bias)],
        out_specs=[pl.BlockSpec((tm * TILE_ROWS, LANES), lambda i: (i, 0)),
                   pl.BlockSpec((REC_WIDTH, tm), lambda i: (0, i)),
                   pl.BlockSpec((1, LANES), lambda i: (0, 0))],
        out_shape=[jax.ShapeDtypeStruct((t * TILE_ROWS, LANES), U32),
                   jax.ShapeDtypeStruct((REC_WIDTH, t), F32),
                   jax.ShapeDtypeStruct((1, LANES), F32)],
        scratch_shapes=[pltpu.VMEM((1, LANES), F32)],
        compiler_params=_cparams(("arbitrary",)),
        name="moe_router",
    )(h, g, w_hi, w_lo, bias)


def _row_copy(src, src_row, dst, dst_row, sem):
    tile = lambda ref, row: ref.at[pl.ds(pl.multiple_of(row * TILE_ROWS, TILE_ROWS), TILE_ROWS)]
    return pltpu.make_async_copy(tile(src, src_row), tile(dst, dst_row), sem)


def _for_each_row(tm, n_pos, fn):
    rows_per_trip = ROW_DMA_UNROLL // n_pos

    def trip(i, carry):
        for j in range(rows_per_trip):
            for k in range(n_pos):
                fn(i * rows_per_trip + j, k, j * n_pos + k)
        return carry

    lax.fori_loop(0, tm // rows_per_trip, trip, 0)


def _dispatch_kernel(pos_ref, xp_ref, zeros_hbm, xs_hbm, sem, *, tm, n_pos):
    del zeros_hbm
    _for_each_row(tm, n_pos, lambda r, k, n: _row_copy(
        xp_ref, r, xs_hbm, pos_ref[0, 0, k * tm + r], sem.at[0]).start(priority=n % 2))
    _for_each_row(tm, n_pos, lambda r, k, n: _row_copy(xp_ref, 0, xs_hbm, 0, sem.at[0]).wait())


def _dispatch(pos, xp, n_slots):
    t = xp.shape[0] // TILE_ROWS
    tm = _row_tile(t, 256)
    n_pos = pos.shape[2] // tm
    return pl.pallas_call(
        functools.partial(_dispatch_kernel, tm=tm, n_pos=n_pos),
        grid=(t // tm,),
        in_specs=[pl.BlockSpec((1, 1, n_pos * tm), lambda i: (i, 0, 0), memory_space=pltpu.SMEM),
                  pl.BlockSpec((tm * TILE_ROWS, LANES), lambda i: (i, 0)),
                  pl.BlockSpec(memory_space=pl.ANY)],
        out_specs=pl.BlockSpec(memory_space=pl.ANY),
        out_shape=jax.ShapeDtypeStruct((n_slots * TILE_ROWS, LANES), U32),
        input_output_aliases={2: 0},
        scratch_shapes=[pltpu.SemaphoreType.DMA((1,))],
        compiler_params=_cparams(("arbitrary",)),
        name="moe_dispatch",
    )(pos, xp, jnp.zeros((n_slots * TILE_ROWS, LANES), U32))


def _expert_kernel(*refs, n_exp, n_blocks):
    be_ref, nb_ref, xs_ref = refs[:3]
    w_refs = refs[3:3 + 3 * n_exp]
    out_ref = refs[3 + 3 * n_exp]
    scratch = refs[4 + 3 * n_exp:]
    b = pl.program_id(0)

    @pl.when(b < nb_ref[0])
    def _():
        prev = jnp.maximum(b - 1, 0)
        for j in range(n_exp):
            wg_ref, wu_ref, wd_ref = w_refs[3 * j:3 * j + 3]
            wgu_s, wd_s = scratch[2 * j:2 * j + 2]

            @pl.when((b == 0) | (be_ref[j * n_blocks + b] != be_ref[j * n_blocks + prev]))
            def _():
                wgu_s[:, :D_EXPERT] = wg_ref[0, 0].astype(BF16)
                wgu_s[:, D_EXPERT:] = wu_ref[0, 0].astype(BF16)
                wd_s[...] = wd_ref[0, 0].astype(BF16)

        rows = _load_tiled_rows(xs_ref, MOE_BLOCK, IN_PIECES)
        lo, hi = _unpack_halves(rows[:, :HALF_D])
        x_lo = lo.astype(BF16)
        x_hi = hi.astype(BF16)
        meta = lax.bitcast_convert_type(rows[:, HALF_D:], F32)
        y = None
        for j in range(n_exp):
            wgu_s, wd_s = scratch[2 * j:2 * j + 2]
            gu = (jnp.dot(x_lo, wgu_s[:HALF_D, :], preferred_element_type=F32)
                  + jnp.dot(x_hi, wgu_s[HALF_D:, :], preferred_element_type=F32))
            gate = gu[:, :D_EXPERT]
            if n_exp == 2:
                w = meta[:, META_WA + j:META_WA + j + 1]
            else:
                this = be_ref[b].astype(F32)
                w = jnp.where(meta[:, META_EA:META_EA + 1] == this,
                              meta[:, META_WA:META_WA + 1], meta[:, META_WB:META_WB + 1])
            hid = (w * (gate * jax.nn.sigmoid(gate) * gu[:, D_EXPERT:])).astype(BF16)
            y_j = jnp.dot(hid, wd_s[...], preferred_element_type=F32)
            y = y_j if y is None else y + y_j
        _store_tiled_rows(out_ref, y)

    @pl.when(b >= nb_ref[0])
    def _():
        out_ref[...] = jnp.zeros_like(out_ref)


def _experts(block_experts, n_blocks_used, xs, w_gate, w_up, w_down, layer, n_exp):
    n_slots = xs.shape[0] // TILE_ROWS
    n_blocks = n_slots // MOE_BLOCK
    block_rows = MOE_BLOCK * TILE_ROWS

    def used(b, nb):
        return jnp.minimum(b, nb[0] - 1)

    def w_spec(shape, j):
        return pl.BlockSpec((1, 1) + shape, lambda b, be, nb: (layer, be[j * n_blocks + used(b, nb)], 0, 0))

    w_specs, w_args, scratch = [], [], []
    for j in range(n_exp):
        w_specs += [w_spec((D_MODEL, D_EXPERT), j), w_spec((D_MODEL, D_EXPERT), j), w_spec((D_EXPERT, D_MODEL), j)]
        w_args += [w_gate, w_up, w_down]
        scratch += [pltpu.VMEM((D_MODEL, 2 * D_EXPERT), BF16), pltpu.VMEM((D_EXPERT, D_MODEL), BF16)]
    grid_spec = pltpu.PrefetchScalarGridSpec(
        num_scalar_prefetch=2,
        grid=(n_blocks,),
        in_specs=[pl.BlockSpec((block_rows, LANES), lambda b, be, nb: (used(b, nb), 0))] + w_specs,
        out_specs=pl.BlockSpec((block_rows, LANES), lambda b, be, nb: (b, 0)),
        scratch_shapes=scratch,
    )
    return pl.pallas_call(
        functools.partial(_expert_kernel, n_exp=n_exp, n_blocks=n_blocks),
        grid_spec=grid_spec,
        out_shape=jax.ShapeDtypeStruct((n_slots * TILE_ROWS, LANES), F32),
        compiler_params=_cparams(("arbitrary",)),
        name="moe_experts",
    )(block_experts, n_blocks_used, xs, *w_args)


def _combine_kernel(*refs, tm, n_pos):
    pos_ref, h_ref, out_hbm, y_ref = refs[:4]
    gathered = refs[4:4 + n_pos]
    sem = refs[4 + n_pos]
    _for_each_row(tm, n_pos, lambda r, k, n: _row_copy(
        out_hbm, pos_ref[0, 0, k * tm + r], gathered[k], r, sem.at[0]).start(priority=n % 2))
    _for_each_row(tm, n_pos, lambda r, k, n: _row_copy(out_hbm, 0, gathered[k], 0, sem.at[0]).wait())

    y = h_ref[...]
    for g_s in gathered:
        y = y + _load_tiled_rows(g_s, tm, range(ROW_PIECES))
    y_ref[...] = y


def _combine(pos, h, out_slots):
    t = h.shape[0]
    tm = _row_tile(t, 256)
    n_pos = pos.shape[2] // tm
    return pl.pallas_call(
        functools.partial(_combine_kernel, tm=tm, n_pos=n_pos),
        grid=(t // tm,),
        in_specs=[pl.BlockSpec((1, 1, n_pos * tm), lambda i: (i, 0, 0), memory_space=pltpu.SMEM),
                  pl.BlockSpec((tm, D_MODEL), lambda i: (i, 0)),
                  pl.BlockSpec(memory_space=pl.ANY)],
        out_specs=pl.BlockSpec((tm, D_MODEL), lambda i: (i, 0)),
        out_shape=jax.ShapeDtypeStruct((t, D_MODEL), F32),
        scratch_shapes=[pltpu.VMEM((tm * TILE_ROWS, LANES), F32)] * n_pos + [pltpu.SemaphoreType.DMA((1,))],
        compiler_params=_cparams(("arbitrary",)),
        name="moe_combine",
    )(pos, h, out_slots)


def _pair_tables():
    pairs = [(a, b) for a in range(EXPERTS_PER_GROUP) for b in range(a + 1, EXPERTS_PER_GROUP)]
    lo = [EXPERTS_PER_GROUP * g + a for g in range(N_GROUPS) for a, _ in pairs]
    hi = [EXPERTS_PER_GROUP * g + b for g in range(N_GROUPS) for _, b in pairs]
    return jnp.array(lo, I32), jnp.array(hi, I32)


def _bins_are_pairs(t):
    return t // MOE_BLOCK >= 2 * N_PAIRS


def _moe(h, p, w_gate, w_up, w_down, layer):
    t = h.shape[0]
    by_pair = _bins_are_pairs(t)
    n_bins, n_pos, n_exp = (N_PAIRS, 1, 2) if by_pair else (N_EXPERTS, 2, 1)
    xp, rec_t, cnt = _router(h, p["g"], p["w_r_hi"], p["w_r_lo"], p["b_r"], by_pair)

    counts = cnt[0, :n_bins].astype(I32)
    blocks_per_bin = (counts + MOE_BLOCK - 1) // MOE_BLOCK
    block_end = jnp.cumsum(blocks_per_bin)
    slot_base = (block_end - blocks_per_bin) * MOE_BLOCK
    n_blocks = n_pos * t // MOE_BLOCK + n_bins
    n_slots = n_blocks * MOE_BLOCK
    block_bin = jnp.minimum(
        jnp.sum(block_end[None, :] <= jnp.arange(n_blocks, dtype=I32)[:, None], axis=1), n_bins - 1).astype(I32)
    n_used = block_end[-1:].astype(I32)
    bin_ids = jnp.arange(n_bins, dtype=F32)[:, None]
    base_of = lambda bin_row: jnp.sum(jnp.where(bin_row[None, :] == bin_ids, slot_base[:, None], 0), axis=0)
    tm = _row_tile(t, 256)
    if by_pair:
        pair_lo, pair_hi = _pair_tables()
        block_experts = jnp.concatenate([pair_lo[block_bin], pair_hi[block_bin]])
        pos = (base_of(rec_t[REC_PAIR]) + rec_t[REC_RANK_A].astype(I32)).reshape(t // tm, 1, tm)
    else:
        block_experts = block_bin
        pos = jnp.concatenate(
            [(base_of(rec_t[REC_EA]) + rec_t[REC_RANK_A].astype(I32)).reshape(t // tm, 1, tm),
             (base_of(rec_t[REC_EB]) + rec_t[REC_RANK_B].astype(I32)).reshape(t // tm, 1, tm)], axis=2)

    xs = _dispatch(pos, xp, n_slots)
    out_slots = _experts(block_experts, n_used, xs, w_gate, w_up, w_down, layer, n_exp)
    return _combine(pos, h, out_slots)


def _fox_in_kernel(h_ref, gkv_ref, gmix_ref, wk_ref, wv_ref, wf_ref, bf_ref, wq_ref, wg_ref,
                   gk_ref, gq_ref, hsum_ref, hexp_ref, k_ref, v_ref, lf_ref, q_ref, sg_ref, *, tm):
    x = h_ref[...]
    xn = x * _rms_scale(x)
    x_kv = (xn * gkv_ref[...]).astype(BF16)
    x_q = (xn * gmix_ref[...]).astype(BF16)
    def head_norm(y):
        ms = jnp.dot((y * y).astype(BF16), hsum_ref[...], preferred_element_type=F32) * (1.0 / FOX_DH)
        scale = lax.rsqrt(ms + EPS).astype(BF16)
        return y * jnp.dot(scale, hexp_ref[...], preferred_element_type=F32)

    k = jnp.dot(x_kv, wk_ref[...], preferred_element_type=F32)
    k_ref[...] = head_norm(k) * gk_ref[...]
    v_ref[...] = jnp.dot(x_kv, wv_ref[...], preferred_element_type=F32)
    fl = jnp.dot(x_kv, wf_ref[...], preferred_element_type=F32) + bf_ref[...]
    lane = lax.broadcasted_iota(I32, (tm, LANES), 1)
    lf_ref[...] = jnp.where(lane < FOX_HEADS, jax.nn.log_sigmoid(fl), 0.0)
    q = jnp.dot(x_q, wq_ref[...], preferred_element_type=F32)
    q_ref[...] = (head_norm(q) * gq_ref[...]).astype(BF16)
    gate = jnp.dot(x_q, wg_ref[...], preferred_element_type=F32)
    sg_ref[...] = jax.nn.sigmoid(gate).astype(BF16)


def _fox_in_proj(h, p):
    t = h.shape[0]
    tm = _row_tile(t, 512)
    row = lambda n: pl.BlockSpec((tm, n), lambda i: (i, 0))
    full = _resident
    consts = (p["g_kv"], p["g_mix"], p["w_k"], p["w_v"], p["w_f"], p["b_f"], p["w_q"], p["w_g"],
              p["g_k"], p["g_q"], p["head_sum"], p["head_expand"])
    return pl.pallas_call(
        functools.partial(_fox_in_kernel, tm=tm),
        grid=(t // tm,),
        in_specs=[row(D_MODEL)] + [full(a) for a in consts],
        out_specs=[row(D_MODEL), row(D_MODEL), row(LANES), row(D_MODEL), row(D_MODEL)],
        out_shape=[jax.ShapeDtypeStruct((t, D_MODEL), F32), jax.ShapeDtypeStruct((t, D_MODEL), F32),
                   jax.ShapeDtypeStruct((t, LANES), F32), jax.ShapeDtypeStruct((t, D_MODEL), BF16),
                   jax.ShapeDtypeStruct((t, D_MODEL), BF16)],
        compiler_params=_cparams(("parallel",)),
        name="fox_in_proj",
    )(h, *consts)


CUMSUM_ROWS = 128


def _cumsum_kernel(x_ref, c_ref, *, seq):
    r_i = lax.broadcasted_iota(I32, (CUMSUM_ROWS, CUMSUM_ROWS), 0)
    c_i = lax.broadcasted_iota(I32, (CUMSUM_ROWS, CUMSUM_ROWS), 1)
    tri = jnp.where(c_i <= r_i, 1.0, 0.0).astype(BF16)

    def body(g, carry):
        rows = pl.ds(pl.multiple_of(g * CUMSUM_ROWS, CUMSUM_ROWS), CUMSUM_ROWS)
        x = x_ref[0, rows, :]
        x1 = x.astype(BF16)
        rem = x - x1.astype(F32)
        x2 = rem.astype(BF16)
        x3 = (rem - x2.astype(F32)).astype(BF16)
        c = (jnp.dot(tri, x1, preferred_element_type=F32) + jnp.dot(tri, x2, preferred_element_type=F32)
             + jnp.dot(tri, x3, preferred_element_type=F32)) + carry
        c_ref[0, rows, :] = c
        return c[CUMSUM_ROWS - 1:CUMSUM_ROWS, :]

    lax.fori_loop(0, seq // CUMSUM_ROWS, body, jnp.zeros((1, LANES), F32))


def _cumsum(x):
    batch, seq, _ = x.shape
    assert seq % CUMSUM_ROWS == 0
    spec = pl.BlockSpec((1, seq, LANES), lambda b: (b, 0, 0))
    return pl.pallas_call(
        functools.partial(_cumsum_kernel, seq=seq),
        grid=(batch,),
        in_specs=[spec],
        out_specs=spec,
        out_shape=jax.ShapeDtypeStruct(x.shape, F32),
        compiler_params=_cparams(("parallel",)),
        name="logf_cumsum",
    )(x)


LOG2E = 1.4426950408889634
KEY_ALIGN = 128


def _split3(x):
    x1 = x.astype(BF16).astype(F32)
    r = x - x1
    x2 = r.astype(BF16).astype(F32)
    return x1, x2, r - x2


def _fox_attn_kernel(q_ref, k_ref, v_ref, c_ref, sg_ref, o_ref, *, seq_q, tq, n_past, n_keys):
    hp = pl.program_id(1)
    lane_k = lax.broadcasted_iota(I32, (n_keys, LANES), 1)
    lane_q = lax.broadcasted_iota(I32, (seq_q, LANES), 1)
    c_pair = c_ref[0] * LOG2E
    k_pair = k_ref[...]
    v_pair = v_ref[...]
    q_pair = q_ref[...].astype(F32)

    k_aug, v_aug, q_aug, den_lane = [], [], [], []
    for hh in range(2):
        own_lo = hh * FOX_DH
        other = FOX_DH - own_lo
        c_k = jnp.sum(jnp.where(lane_k == 2 * hp + hh, c_pair, 0.0), axis=1, keepdims=True)
        c1, c2, c3 = _split3(c_k)
        d_k = lane_k - other
        extra_k = jnp.where((d_k >= 0) & (d_k < 3), 1.0,
                            jnp.where(d_k == 3, -c1, jnp.where(d_k == 4, -c2, jnp.where(d_k == 5, -c3, 0.0))))
        own_k = (lane_k >= own_lo) & (lane_k < own_lo + FOX_DH)
        k_aug.append(jnp.where(own_k, k_pair, extra_k).astype(BF16))
        v_aug.append(jnp.where(own_k, v_pair, jnp.where(d_k == 0, 1.0, 0.0)).astype(BF16))
        q1, q2, q3 = (c[n_past:n_past + seq_q] for c in (c1, c2, c3))
        d_q = lane_q - other
        extra_q = jnp.where(d_q == 0, q1, jnp.where(d_q == 1, q2, jnp.where(d_q == 2, q3,
                            jnp.where((d_q >= 3) & (d_q < 6), 1.0, 0.0))))
        own_q = (lane_q >= own_lo) & (lane_q < own_lo + FOX_DH)
        q_aug.append(jnp.where(own_q, q_pair, extra_q).astype(BF16))
        den_lane.append(other)

    nt = (((1,), (1,)), ((), ()))
    lane_o = lax.broadcasted_iota(I32, (tq, LANES), 1)
    for r0 in range(0, seq_q, tq):
        first_q = n_past + r0
        n_full = (first_q + 1) // KEY_ALIGN * KEY_ALIGN
        n_vis = min(-(-(first_q + tq) // KEY_ALIGN) * KEY_ALIGN, n_keys)
        k_pos = n_full + lax.broadcasted_iota(I32, (tq, n_vis - n_full), 1)
        q_pos = first_q + lax.broadcasted_iota(I32, (tq, n_vis - n_full), 0)
        visible = k_pos <= q_pos
        heads = []
        for hh in range(2):
            qa = q_aug[hh][r0:r0 + tq]
            s_edge = lax.dot_general(qa, k_aug[hh][n_full:n_vis], nt, preferred_element_type=F32)
            s_edge = jnp.where(visible, s_edge, -jnp.inf)
            m = jnp.max(s_edge, axis=1, keepdims=True)
            if n_full:
                s_full = lax.dot_general(qa, k_aug[hh][:n_full], nt, preferred_element_type=F32)
                m = jnp.maximum(m, jnp.max(s_full, axis=1, keepdims=True))
            acc = jnp.dot(jnp.exp2(s_edge - m).astype(BF16), v_aug[hh][n_full:n_vis],
                          preferred_element_type=F32)
            if n_full:
                acc = acc + jnp.dot(jnp.exp2(s_full - m).astype(BF16), v_aug[hh][:n_full],
                                    preferred_element_type=F32)
            denom = jnp.sum(jnp.where(lane_o == den_lane[hh], acc, 0.0), axis=1, keepdims=True)
            heads.append(acc / denom)
        o = jnp.where(lane_o < FOX_DH, heads[0], heads[1])
        o_ref[r0:r0 + tq, :] = (o * sg_ref[r0:r0 + tq, :].astype(F32)).astype(BF16)


def _fox_attention(q, sg, k_all, v_all, c_all, batch, seq_q, n_keys, n_past, tq):
    assert seq_q % tq == 0 and n_keys % KEY_ALIGN == 0
    kern = functools.partial(_fox_attn_kernel, seq_q=seq_q, tq=tq, n_past=n_past, n_keys=n_keys)
    return pl.pallas_call(
        kern,
        grid=(batch, FOX_PAIRS),
        in_specs=[pl.BlockSpec((seq_q, LANES), lambda b, hp: (b, hp)),
                  pl.BlockSpec((n_keys, LANES), lambda b, hp: (b, hp)),
                  pl.BlockSpec((n_keys, LANES), lambda b, hp: (b, hp)),
                  pl.BlockSpec((1, n_keys, LANES), lambda b, hp: (b, 0, 0)),
                  pl.BlockSpec((seq_q, LANES), lambda b, hp: (b, hp))],
        out_specs=pl.BlockSpec((seq_q, LANES), lambda b, hp: (b, hp)),
        out_shape=jax.ShapeDtypeStruct((batch * seq_q, D_MODEL), BF16),
        compiler_params=_cparams(("parallel", "parallel")),
        name="fox_attention",
    )(q, k_all, v_all, c_all, sg)


def _pad_cols(w, n):
    return jnp.pad(w, ((0, 0), (0, n - w.shape[1])))


def _prepare(norm_mix, norm_ffn, w_gla_in, w_gla_gate2, b_gla_gate, g_gla_head, w_gla_out, g_kv, w_kv,
             b_forget, g_k, w_fox_qg, g_q, w_fox_out, w_group, b_group, w_router, b_router):
    row = lambda a: a.reshape(1, -1).astype(F32)
    w_in = w_gla_in[0]
    c0, c1, c2, c3 = 2 * GLA_QK, 2 * GLA_QK + GLA_V, 2 * GLA_QK + GLA_V + GLA_GATE_RANK, w_in.shape[1]
    gla = dict(
        g=row(norm_mix[0]),
        w_qk=w_in[:, :c0].astype(BF16), w_v=w_in[:, c0:c1].astype(BF16), w_r=w_in[:, c2:c3].astype(BF16),
        w_g=_pad_cols(w_in[:, c1:c2], LANES).astype(BF16),
        w_g2=jnp.pad(w_gla_gate2[0], ((0, LANES - GLA_GATE_RANK), (0, 0))).astype(BF16),
        b_g=row(b_gla_gate[0]), g_head=row(g_gla_head[0]), w_out=w_gla_out[0].astype(BF16),
    )
    head_id = jnp.arange(D_MODEL) // FOX_DH
    fox = dict(
        g_kv=row(g_kv), g_mix=row(norm_mix[1]),
        w_k=w_kv[:, :D_MODEL].astype(BF16), w_v=w_kv[:, D_MODEL:2 * D_MODEL].astype(BF16),
        w_f=_pad_cols(w_kv[:, 2 * D_MODEL:], LANES).astype(BF16),
        b_f=_pad_cols(row(b_forget), LANES),
        w_q=w_fox_qg[0][:, :D_MODEL].astype(BF16), w_g=w_fox_qg[0][:, D_MODEL:].astype(BF16),
        g_k=row(jnp.tile(g_k, FOX_HEADS)), g_q=row(jnp.tile(g_q[0], FOX_HEADS)) * (FOX_DH ** -0.5 * LOG2E),
        head_sum=(head_id[:, None] == jnp.arange(LANES)[None, :]).astype(BF16),
        head_expand=(jnp.arange(LANES)[:, None] == head_id[None, :]).astype(BF16),
        w_out=w_fox_out[0].astype(BF16),
    )
    moe = []
    for layer in range(2):
        w_r = _pad_cols(jnp.concatenate([w_group[layer], w_router[layer]], axis=1), LANES)
        w_r_hi = w_r.astype(BF16)
        moe.append(dict(
            g=row(norm_ffn[layer]),
            w_r_hi=w_r_hi, w_r_lo=(w_r - w_r_hi.astype(F32)).astype(BF16),
            b_r=_pad_cols(jnp.concatenate([row(b_group[layer]), row(b_router[layer])], axis=1), LANES),
        ))
    return gla, fox, moe


def _trunk(x, s0, past, gla, fox, moe, experts):
    batch, seq, _ = x.shape
    t = batch * seq
    h = x.reshape(t, D_MODEL)

    q, k, v, r, la = _gla_in_proj(h, gla["g"], gla["w_qk"], gla["w_v"], gla["w_r"], gla["w_g"], gla["w_g2"],
                                  gla["b_g"])
    o, s_new = _gla(q, k, v, la, s0, gla["g_head"], batch, seq)
    h = _out_proj(o, r, gla["w_out"], h)
    h = _moe(h, moe[0], *experts, 0)

    k_new, v_new, lf_new, qf, sg = _fox_in_proj(h, fox)
    lf3 = lf_new.reshape(batch, seq, LANES)
    if past is None:
        n_past, n_keys = 0, seq
        tq = min(seq, 512)
        k_all, v_all, lf_all = k_new, v_new, lf3
    else:
        past_k, past_v, past_lf = past
        n_past = past_k.shape[1]
        tq = seq
        n_keys = -(-(n_past + seq) // KEY_ALIGN) * KEY_ALIGN
        pad = n_keys - n_past - seq
        cat = lambda a, b: jnp.concatenate(
            [a, b, jnp.zeros((batch, pad, a.shape[2]), F32)], axis=1)
        k_all = cat(past_k.reshape(batch, n_past, D_MODEL), k_new.reshape(batch, seq, D_MODEL)
                    ).reshape(batch * n_keys, D_MODEL)
        v_all = cat(past_v.reshape(batch, n_past, D_MODEL), v_new.reshape(batch, seq, D_MODEL)
                    ).reshape(batch * n_keys, D_MODEL)
        lf_all = cat(jnp.pad(past_lf, ((0, 0), (0, 0), (0, LANES - FOX_HEADS))), lf3)
    c_all = _cumsum(lf_all)
    o = _fox_attention(qf, sg, k_all, v_all, c_all, batch, seq, n_keys, n_past, tq)
    h = _out_proj(o, None, fox["w_out"], h)
    h = _moe(h, moe[1], *experts, 1)

    return (h.reshape(batch, seq, D_MODEL),
            k_new.reshape(batch, seq, FOX_HEADS, FOX_DH),
            v_new.reshape(batch, seq, FOX_HEADS, FOX_DH),
            lf3[:, :, :FOX_HEADS],
            s_new[None])


def kernel(x_prompt, x_sample, cache_k, cache_v, cache_logf, state_gla, norm_mix, norm_ffn, w_gla_in, w_gla_gate2, b_gla_gate, g_gla_head, w_gla_out, g_kv, w_kv, b_forget, g_k, w_fox_qg, g_q, w_fox_out, w_group, b_group, w_router, b_router, w_exp_gate, w_exp_up, w_exp_down):
    gla, fox, moe = _prepare(norm_mix, norm_ffn, w_gla_in, w_gla_gate2, b_gla_gate, g_gla_head, w_gla_out, g_kv,
                             w_kv, b_forget, g_k, w_fox_qg, g_q, w_fox_out, w_group, b_group, w_router, b_router)
    experts = (w_exp_gate.astype(F32), w_exp_up.astype(F32), w_exp_down.astype(F32))
    s_zero = jnp.zeros((x_prompt.shape[0], GLA_HEADS, GLA_DK, GLA_DV), F32)
    y_p, k_p, v_p, lf_p, s_p = _trunk(x_prompt, s_zero, None, gla, fox, moe, experts)
    y_s, k_s, v_s, lf_s, s_s = _trunk(x_sample, state_gla[0].astype(F32),
                                      (cache_k.astype(F32), cache_v.astype(F32), cache_logf.astype(F32)),
                                      gla, fox, moe, experts)
    return (y_p, y_s, k_p, v_p, lf_p, s_p, k_s, v_s, lf_s, s_s)
```

```python
import functools

import jax
import jax.numpy as jnp
from jax import lax
from jax.experimental import pallas as pl
from jax.experimental.pallas import tpu as pltpu

F32 = jnp.float32
BF16 = jnp.bfloat16
U32 = jnp.uint32
I32 = jnp.int32

EPS = 1e-6
D_MODEL = 1024
HALF_D = D_MODEL // 2
LANES = 128
VMEM_LIMIT_BYTES = 56 * 1024 * 1024

GLA_HEADS = 4
GLA_DK = 128
GLA_DV = 256
GLA_QK = GLA_HEADS * GLA_DK
GLA_V = GLA_HEADS * GLA_DV
GLA_GATE_RANK = 16
GLA_GATE_TAU = 16.0
GLA_CHUNK = 64
GLA_GROUP = 256

FOX_HEADS = 16
FOX_DH = 64
FOX_PAIRS = FOX_HEADS // 2

N_GROUPS = 4
EXPERTS_PER_GROUP = 8
N_EXPERTS = N_GROUPS * EXPERTS_PER_GROUP
D_EXPERT = 512
MOE_BLOCK = 256
ROW_DMA_UNROLL = 16


def _cparams(sem):
    return pltpu.CompilerParams(dimension_semantics=sem, vmem_limit_bytes=VMEM_LIMIT_BYTES)


def _row_tile(t, pref):
    tm = min(t, pref)
    assert t % tm == 0
    return tm


def _resident(a):
    return pl.BlockSpec(a.shape, lambda *_: (0,) * a.ndim, pipeline_mode=pl.Buffered(1))


def _rms_scale(x):
    return lax.rsqrt(jnp.mean(x * x, axis=-1, keepdims=True) + EPS)


def _pack_halves(y):
    lo = lax.bitcast_convert_type(y[:, :HALF_D].astype(BF16).astype(F32), U32)
    hi = lax.bitcast_convert_type(y[:, HALF_D:].astype(BF16).astype(F32), U32)
    return (lo >> 16) | (hi & jnp.uint32(0xFFFF0000))


def _unpack_halves(p):
    lo = lax.bitcast_convert_type(p << 16, F32)
    hi = lax.bitcast_convert_type(p & jnp.uint32(0xFFFF0000), F32)
    return lo, hi


TILE_ROWS = 8
ROW_PIECES = D_MODEL // LANES
assert ROW_PIECES == TILE_ROWS


def _load_tiled_rows(ref, n, pieces):
    return jnp.concatenate([ref[pl.ds(j, n, stride=TILE_ROWS), :] for j in pieces], axis=1)


def _store_tiled_rows(ref, val):
    n = val.shape[0]
    for j in range(ROW_PIECES):
        ref[pl.ds(j, n, stride=TILE_ROWS), :] = val[:, j * LANES:(j + 1) * LANES]


def _gla_in_kernel(h_ref, g_ref, wqk_ref, wv_ref, wr_ref, wg_ref, wg2_ref, bg_ref,
                   q_ref, k_ref, v_ref, r_ref, la_ref):
    x = h_ref[...]
    hn = (x * _rms_scale(x) * g_ref[...]).astype(BF16)
    qk = jnp.dot(hn, wqk_ref[...], preferred_element_type=F32)
    q_ref[...] = (qk[:, :GLA_QK] * (GLA_DK ** -0.5)).astype(BF16)
    k_ref[...] = qk[:, GLA_QK:].astype(BF16)
    v_ref[...] = jnp.dot(hn, wv_ref[...], preferred_element_type=F32).astype(BF16)
    r_ref[...] = jnp.dot(hn, wr_ref[...], preferred_element_type=F32).astype(BF16)
    g_lr = jnp.dot(hn, wg_ref[...], preferred_element_type=F32)
    z = jnp.dot(g_lr.astype(BF16), wg2_ref[...], preferred_element_type=F32) + bg_ref[...]
    la_ref[...] = jax.nn.log_sigmoid(z) / GLA_GATE_TAU


def _gla_in_proj(h, g, wqk, wv, wr, wg, wg2, bg):
    t = h.shape[0]
    tm = _row_tile(t, 512)
    row = lambda n: pl.BlockSpec((tm, n), lambda i: (i, 0))
    full = _resident
    return pl.pallas_call(
        _gla_in_kernel,
        grid=(t // tm,),
        in_specs=[row(D_MODEL), full(g), full(wqk), full(wv), full(wr), full(wg), full(wg2), full(bg)],
        out_specs=[row(GLA_QK), row(GLA_QK), row(GLA_V), row(GLA_V), row(GLA_QK)],
        out_shape=[jax.ShapeDtypeStruct((t, GLA_QK), BF16), jax.ShapeDtypeStruct((t, GLA_QK), BF16),
                   jax.ShapeDtypeStruct((t, GLA_V), BF16), jax.ShapeDtypeStruct((t, GLA_V), BF16),
                   jax.ShapeDtypeStruct((t, GLA_QK), F32)],
        compiler_params=_cparams(("parallel",)),
        name="gla_in_proj",
    )(h, g, wqk, wv, wr, wg, wg2, bg)


def _gla_kernel(q_ref, k_ref, v_ref, la_ref, s0_ref, gh_ref, o_ref, s_ref,
                qb_s, oi_s, u_s, dl_s, *, seq, chunk, group):
    n_groups = seq // group
    n_chunks = seq // chunk
    per_group = group // chunk

    shift = chunk.bit_length() - 1
    assert chunk == 1 << shift and group & (group - 1) == 0
    row = lax.broadcasted_iota(I32, (group, group), 0)
    col = lax.broadcasted_iota(I32, (group, group), 1)
    tril = ((row >> shift) == (col >> shift)) & (col <= row)
    tril_bf = jnp.where(tril, 1.0, 0.0).astype(BF16)

    def chunk_rows(b, at):
        return jnp.concatenate(
            [jnp.broadcast_to(b[j * chunk + at:j * chunk + at + 1], (chunk, GLA_DK)) for j in range(per_group)],
            axis=0)

    def group_body(g, carry):
        r0 = pl.multiple_of(g * group, group)
        rows = pl.ds(r0, group)
        la = la_ref[rows, :]
        la_hi = la.astype(BF16)
        la_lo = (la - la_hi.astype(F32)).astype(BF16)
        b = (jnp.dot(tril_bf, la_hi, preferred_element_type=F32)
             + jnp.dot(tril_bf, la_lo, preferred_element_type=F32))
        b_ref = chunk_rows(b, chunk // 2)
        b_last = chunk_rows(b, chunk - 1)
        qf = q_ref[rows, :].astype(F32)
        kf = k_ref[rows, :].astype(F32)
        vg = v_ref[rows, :]
        qe = (qf * jnp.exp(b - b_ref)).astype(BF16)
        ke = (kf * jnp.exp(b_ref - b)).astype(BF16)
        att = lax.dot_general(qe, ke, (((1,), (1,)), ((), ())), preferred_element_type=F32)
        att = jnp.where(tril, att, 0.0).astype(BF16)
        oi_s[rows, :] = jnp.dot(att, vg, preferred_element_type=F32)
        qb_s[rows, :] = (qf * jnp.exp(b)).astype(BF16)
        kl = (kf * jnp.exp(b_last - b)).astype(BF16)
        decay = jnp.exp(b_last)
        for j in range(per_group):
            c = g * per_group + j
            sl = slice(j * chunk, (j + 1) * chunk)
            u_s[c] = lax.dot_general(vg[sl], kl[sl], (((0,), (0,)), ((), ())), preferred_element_type=F32)
            dl_s[c] = decay[j * chunk:j * chunk + 1]
        return carry

    lax.fori_loop(0, n_groups, group_body, 0, unroll=min(2, n_groups))

    gh = gh_ref[...]

    def chunk_body(c, st):
        r0 = pl.multiple_of(c * chunk, chunk)
        rows = pl.ds(r0, chunk)
        o = oi_s[rows, :] + lax.dot_general(qb_s[rows, :], st.astype(BF16), (((1,), (1,)), ((), ())),
                                            preferred_element_type=F32)
        o_ref[rows, :] = (o * _rms_scale(o) * gh).astype(BF16)
        return st * dl_s[c] + u_s[c]

    st = lax.fori_loop(0, n_chunks, chunk_body, s0_ref[0, 0].T, unroll=min(4, n_chunks))
    s_ref[0, 0] = st.T


def _gla(q, k, v, la, s0, g_head, batch, seq):
    chunk = min(seq, GLA_CHUNK)
    group = min(seq, GLA_GROUP)
    assert seq % group == 0 and group % chunk == 0
    n_chunks = seq // chunk
    kern = functools.partial(_gla_kernel, seq=seq, chunk=chunk, group=group)
    return pl.pallas_call(
        kern,
        grid=(batch, GLA_HEADS),
        in_specs=[pl.BlockSpec((seq, GLA_DK), lambda b, h: (b, h)),
                  pl.BlockSpec((seq, GLA_DK), lambda b, h: (b, h)),
                  pl.BlockSpec((seq, GLA_DV), lambda b, h: (b, h)),
                  pl.BlockSpec((seq, GLA_DK), lambda b, h: (b, h)),
                  pl.BlockSpec((1, 1, GLA_DK, GLA_DV), lambda b, h: (b, h, 0, 0)),
                  pl.BlockSpec((1, GLA_DV), lambda b, h: (0, 0))],
        out_specs=[pl.BlockSpec((seq, GLA_DV), lambda b, h: (b, h)),
                   pl.BlockSpec((1, 1, GLA_DK, GLA_DV), lambda b, h: (b, h, 0, 0))],
        out_shape=[jax.ShapeDtypeStruct((batch * seq, GLA_V), BF16),
                   jax.ShapeDtypeStruct((batch, GLA_HEADS, GLA_DK, GLA_DV), F32)],
        scratch_shapes=[pltpu.VMEM((seq, GLA_DK), BF16),
                        pltpu.VMEM((seq, GLA_DV), F32),
                        pltpu.VMEM((n_chunks, GLA_DV, GLA_DK), F32),
                        pltpu.VMEM((n_chunks, 1, GLA_DK), F32)],
        compiler_params=_cparams(("parallel", "parallel")),
        name="gla_scan",
    )(q, k, v, la, s0, g_head)


def _out_proj_kernel(*refs, gated):
    if gated:
        a_ref, r_ref, w_ref, h_ref, o_ref = refs
        r = r_ref[...].astype(F32)
        a = (a_ref[...].astype(F32) * (r * jax.nn.sigmoid(r))).astype(BF16)
    else:
        a_ref, w_ref, h_ref, o_ref = refs
        a = a_ref[...]
    o_ref[...] = h_ref[...] + jnp.dot(a, w_ref[...], preferred_element_type=F32)


def _out_proj(a, r, w, h):
    t = h.shape[0]
    tm = _row_tile(t, 512)
    row = pl.BlockSpec((tm, D_MODEL), lambda i: (i, 0))
    wspec = _resident(w)
    gated = r is not None
    args = (a, r, w, h) if gated else (a, w, h)
    specs = [row, row, wspec, row] if gated else [row, wspec, row]
    return pl.pallas_call(
        functools.partial(_out_proj_kernel, gated=gated),
        grid=(t // tm,),
        in_specs=specs,
        out_specs=row,
        out_shape=jax.ShapeDtypeStruct((t, D_MODEL), F32),
        compiler_params=_cparams(("parallel",)),
        name="out_proj_gated" if gated else "out_proj",
    )(*args)


REC_EA, REC_EB, REC_PAIR, REC_RANK_A, REC_RANK_B = range(5)
REC_WIDTH = 8
META_WA, META_WB, META_EA, META_EB = range(4)
META_PIECE = HALF_D // LANES
IN_PIECES = range(META_PIECE + 1)
PAIRS_PER_GROUP = EXPERTS_PER_GROUP * (EXPERTS_PER_GROUP - 1) // 2
N_PAIRS = N_GROUPS * PAIRS_PER_GROUP


def _router_kernel(h_ref, g_ref, whi_ref, wlo_ref, bias_ref, xp_ref, rec_t_ref, cnt_ref, carry_s, *, tm, by_pair):
    @pl.when(pl.program_id(0) == 0)
    def _():
        carry_s[...] = jnp.zeros_like(carry_s)

    x = h_ref[...]
    hn = x * _rms_scale(x) * g_ref[...]

    hi = hn.astype(BF16)
    lo = (hn - hi.astype(F32)).astype(BF16)
    w_hi = whi_ref[...]
    lg = (jnp.dot(hi, w_hi, preferred_element_type=F32) + jnp.dot(lo, w_hi, preferred_element_type=F32)
          + jnp.dot(hi, wlo_ref[...], preferred_element_type=F32) + bias_ref[...])

    lane = lax.broadcasted_iota(I32, (tm, LANES), 1).astype(F32)
    neg = jnp.float32(-jnp.inf)

    def masked_softmax(mask):
        m = jnp.max(jnp.where(mask, lg, neg), axis=1, keepdims=True)
        e = jnp.where(mask, jnp.exp(lg - m), 0.0)
        return e / jnp.sum(e, axis=1, keepdims=True)

    def top1(p, mask):
        v = jnp.max(jnp.where(mask, p, -1.0), axis=1, keepdims=True)
        idx = jnp.min(jnp.where(mask & (p == v), lane, float(LANES)), axis=1, keepdims=True)
        return v, idx

    gmask = lane < N_GROUPS
    g_top, g_idx = top1(masked_softmax(gmask), gmask)
    e_lo = N_GROUPS + EXPERTS_PER_GROUP * g_idx
    emask = (lane >= e_lo) & (lane < e_lo + EXPERTS_PER_GROUP)
    ep = masked_softmax(emask)
    p1, i1 = top1(ep, emask)
    mask2 = emask & (lane != i1)
    p2, i2 = top1(ep, mask2)
    denom = p1 + p2
    w0 = g_top * p1 / denom
    w1 = g_top * p2 / denom

    a0 = i1 - e_lo
    a1 = i2 - e_lo
    first_low = a0 < a1
    lo_l = jnp.where(first_low, a0, a1)
    hi_l = jnp.where(first_low, a1, a0)
    w_a = jnp.where(first_low, w0, w1)
    w_b = jnp.where(first_low, w1, w0)
    e_a = EXPERTS_PER_GROUP * g_idx + lo_l
    e_b = EXPERTS_PER_GROUP * g_idx + hi_l
    pair = (PAIRS_PER_GROUP * g_idx + lo_l * (2 * EXPERTS_PER_GROUP - 1 - lo_l) * 0.5 + (hi_l - lo_l - 1.0))

    meta = jnp.zeros((tm, LANES), F32)
    for slot, val in ((META_WA, w_a), (META_WB, w_b), (META_EA, e_a), (META_EB, e_b)):
        meta = jnp.where(lane == slot, val, meta)
    _store_tiled_rows(xp_ref, jnp.concatenate(
        [_pack_halves(hn), lax.bitcast_convert_type(meta, U32),
         jnp.zeros((tm, D_MODEL - HALF_D - LANES), U32)], axis=1))

    if by_pair:
        hit_a = lane == pair
        onehot = jnp.where(hit_a, 1.0, 0.0)
    else:
        hit_a = lane == e_a
        hit_b = lane == e_b
        onehot = jnp.where(hit_a | hit_b, 1.0, 0.0)
    r_i = lax.broadcasted_iota(I32, (tm, tm), 0)
    c_i = lax.broadcasted_iota(I32, (tm, tm), 1)
    before = jnp.where(c_i < r_i, 1.0, 0.0).astype(BF16)
    prior = jnp.dot(before, onehot.astype(BF16), preferred_element_type=F32) + carry_s[...]
    rank_a = jnp.sum(jnp.where(hit_a, prior, 0.0), axis=1, keepdims=True)
    rank_b = jnp.zeros_like(rank_a) if by_pair else jnp.sum(jnp.where(hit_b, prior, 0.0), axis=1, keepdims=True)
    carry_s[...] = carry_s[...] + jnp.sum(onehot, axis=0, keepdims=True)
    cnt_ref[...] = carry_s[...]

    rec = jnp.zeros((tm, LANES), F32)
    for slot, val in ((REC_EA, e_a), (REC_EB, e_b), (REC_PAIR, pair), (REC_RANK_A, rank_a), (REC_RANK_B, rank_b)):
        rec = jnp.where(lane == slot, val, rec)
    rec_t_ref[...] = rec.T[:REC_WIDTH]


def _router(h, g, w_hi, w_lo, bias, by_pair):
    t = h.shape[0]
    tm = _row_tile(t, 512)
    full = _resident
    return pl.pallas_call(
        functools.partial(_router_kernel, tm=tm, by_pair=by_pair),
        grid=(t // tm,),
        in_specs=[pl.BlockSpec((tm, D_MODEL), lambda i: (i, 0)), full(g), full(w_hi), full(w_lo), full(bias)],
        out_specs=[pl.BlockSpec((tm * TILE_ROWS, LANES), lambda i: (i, 0)),
                   pl.BlockSpec((REC_WIDTH, tm), lambda i: (0, i)),
                   pl.BlockSpec((1, LANES), lambda i: (0, 0))],
        out_shape=[jax.ShapeDtypeStruct((t * TILE_ROWS, LANES), U32),
                   jax.ShapeDtypeStruct((REC_WIDTH, t), F32),
                   jax.ShapeDtypeStruct((1, LANES), F32)],
        scratch_shapes=[pltpu.VMEM((1, LANES), F32)],
        compiler_params=_cparams(("arbitrary",)),
        name="moe_router",
    )(h, g, w_hi, w_lo, bias)


def _row_copy(src, src_row, dst, dst_row, sem):
    tile = lambda ref, row: ref.at[pl.ds(pl.multiple_of(row * TILE_ROWS, TILE_ROWS), TILE_ROWS)]
    return pltpu.make_async_copy(tile(src, src_row), tile(dst, dst_row), sem)


def _for_each_row(tm, n_pos, fn):
    rows_per_trip = ROW_DMA_UNROLL // n_pos

    def trip(i, carry):
        for j in range(rows_per_trip):
            for k in range(n_pos):
                fn(i * rows_per_trip + j, k, j * n_pos + k)
        return carry

    lax.fori_loop(0, tm // rows_per_trip, trip, 0)


def _dispatch_kernel(pos_ref, xp_ref, zeros_hbm, xs_hbm, sem, *, tm, n_pos):
    del zeros_hbm
    _for_each_row(tm, n_pos, lambda r, k, n: _row_copy(
        xp_ref, r, xs_hbm, pos_ref[0, 0, k * tm + r], sem.at[0]).start(priority=n % 2))
    _for_each_row(tm, n_pos, lambda r, k, n: _row_copy(xp_ref, 0, xs_hbm, 0, sem.at[0]).wait())


def _dispatch(pos_cols, xp, n_slots):
    t = xp.shape[0] // TILE_ROWS
    tm = _row_tile(t, 256)
    n_pos = len(pos_cols)
    pos = jnp.concatenate([p.reshape(t // tm, 1, tm) for p in pos_cols], axis=2)
    return pl.pallas_call(
        functools.partial(_dispatch_kernel, tm=tm, n_pos=n_pos),
        grid=(t // tm,),
        in_specs=[pl.BlockSpec((1, 1, n_pos * tm), lambda i: (i, 0, 0), memory_space=pltpu.SMEM),
                  pl.BlockSpec((tm * TILE_ROWS, LANES), lambda i: (i, 0)),
                  pl.BlockSpec(memory_space=pl.ANY)],
        out_specs=pl.BlockSpec(memory_space=pl.ANY),
        out_shape=jax.ShapeDtypeStruct((n_slots * TILE_ROWS, LANES), U32),
        input_output_aliases={2: 0},
        scratch_shapes=[pltpu.SemaphoreType.DMA((1,))],
        compiler_params=_cparams(("arbitrary",)),
        name="moe_dispatch",
    )(pos, xp, jnp.zeros((n_slots * TILE_ROWS, LANES), U32))


def _expert_kernel(*refs, n_exp, n_blocks):
    be_ref, nb_ref, xs_ref = refs[:3]
    w_refs = refs[3:3 + 3 * n_exp]
    out_ref = refs[3 + 3 * n_exp]
    scratch = refs[4 + 3 * n_exp:]
    b = pl.program_id(0)

    @pl.when(b < nb_ref[0])
    def _():
        prev = jnp.maximum(b - 1, 0)
        for j in range(n_exp):
            wg_ref, wu_ref, wd_ref = w_refs[3 * j:3 * j + 3]
            wgu_s, wd_s = scratch[2 * j:2 * j + 2]

            @pl.when((b == 0) | (be_ref[j * n_blocks + b] != be_ref[j * n_blocks + prev]))
            def _():
                wgu_s[:, :D_EXPERT] = wg_ref[0, 0].astype(BF16)
                wgu_s[:, D_EXPERT:] = wu_ref[0, 0].astype(BF16)
                wd_s[...] = wd_ref[0, 0].astype(BF16)

        rows = _load_tiled_rows(xs_ref, MOE_BLOCK, IN_PIECES)
        lo, hi = _unpack_halves(rows[:, :HALF_D])
        x_lo = lo.astype(BF16)
        x_hi = hi.astype(BF16)
        meta = lax.bitcast_convert_type(rows[:, HALF_D:], F32)
        y = None
        for j in range(n_exp):
            wgu_s, wd_s = scratch[2 * j:2 * j + 2]
            gu = (jnp.dot(x_lo, wgu_s[:HALF_D, :], preferred_element_type=F32)
                  + jnp.dot(x_hi, wgu_s[HALF_D:, :], preferred_element_type=F32))
            gate = gu[:, :D_EXPERT]
            if n_exp == 2:
                w = meta[:, META_WA + j:META_WA + j + 1]
            else:
                this = be_ref[b].astype(F32)
                w = jnp.where(meta[:, META_EA:META_EA + 1] == this,
                              meta[:, META_WA:META_WA + 1], meta[:, META_WB:META_WB + 1])
            hid = (w * (gate * jax.nn.sigmoid(gate) * gu[:, D_EXPERT:])).astype(BF16)
            y_j = jnp.dot(hid, wd_s[...], preferred_element_type=F32)
            y = y_j if y is None else y + y_j
        _store_tiled_rows(out_ref, y)

    @pl.when(b >= nb_ref[0])
    def _():
        out_ref[...] = jnp.zeros_like(out_ref)


def _experts(block_experts, n_blocks_used, xs, w_gate, w_up, w_down, layer, n_exp):
    n_slots = xs.shape[0] // TILE_ROWS
    n_blocks = n_slots // MOE_BLOCK
    block_rows = MOE_BLOCK * TILE_ROWS

    def used(b, nb):
        return jnp.minimum(b, nb[0] - 1)

    def w_spec(shape, j):
        return pl.BlockSpec((1, 1) + shape, lambda b, be, nb: (layer, be[j * n_blocks + used(b, nb)], 0, 0))

    w_specs, w_args, scratch = [], [], []
    for j in range(n_exp):
        w_specs += [w_spec((D_MODEL, D_EXPERT), j), w_spec((D_MODEL, D_EXPERT), j), w_spec((D_EXPERT, D_MODEL), j)]
        w_args += [w_gate, w_up, w_down]
        scratch += [pltpu.VMEM((D_MODEL, 2 * D_EXPERT), BF16), pltpu.VMEM((D_EXPERT, D_MODEL), BF16)]
    grid_spec = pltpu.PrefetchScalarGridSpec(
        num_scalar_prefetch=2,
        grid=(n_blocks,),
        in_specs=[pl.BlockSpec((block_rows, LANES), lambda b, be, nb: (used(b, nb), 0))] + w_specs,
        out_specs=pl.BlockSpec((block_rows, LANES), lambda b, be, nb: (b, 0)),
        scratch_shapes=scratch,
    )
    return pl.pallas_call(
        functools.partial(_expert_kernel, n_exp=n_exp, n_blocks=n_blocks),
        grid_spec=grid_spec,
        out_shape=jax.ShapeDtypeStruct((n_slots * TILE_ROWS, LANES), F32),
        compiler_params=_cparams(("arbitrary",)),
        name="moe_experts",
    )(block_experts, n_blocks_used, xs, *w_args)


def _combine_kernel(*refs, tm, n_pos, n_fox_consts):
    pos_ref, pos_next_ref, h_ref, out_hbm = refs[:4]
    fox_consts = refs[4:4 + n_fox_consts]
    n_out = 6 if n_fox_consts else 1
    outs = refs[4 + n_fox_consts:4 + n_fox_consts + n_out]
    gathered = refs[4 + n_fox_consts + n_out:-1]
    sem = refs[-1]
    i = pl.program_id(0)
    slot = i % 2

    def issue(p_ref, s):
        _for_each_row(tm, n_pos, lambda r, k, n: _row_copy(
            out_hbm, p_ref[0, 0, k * tm + r], gathered[k].at[s], r, sem.at[s]).start(priority=n % 2))

    @pl.when(i == 0)
    def _():
        issue(pos_ref, 0)

    @pl.when(i + 1 < pl.num_programs(0))
    def _():
        issue(pos_next_ref, 1 - slot)

    _for_each_row(tm, n_pos, lambda r, k, n: _row_copy(
        out_hbm, 0, gathered[k].at[slot], 0, sem.at[slot]).wait())

    y = h_ref[...]
    for g_s in gathered:
        y = y + _load_tiled_rows(g_s.at[slot], tm, range(ROW_PIECES))
    outs[0][...] = y
    if n_fox_consts:
        _fox_in_body(y, *fox_consts, *outs[1:], tm=tm)


def _combine(pos_cols, h, out_slots, fox=None):
    t = h.shape[0]
    tm = _row_tile(t, 512)
    nt = t // tm
    n_pos = len(pos_cols)
    pos = jnp.concatenate([p.reshape(nt, 1, tm) for p in pos_cols], axis=2)
    row = lambda n: pl.BlockSpec((tm, n), lambda i: (i, 0))
    pos_spec = lambda step: pl.BlockSpec((1, 1, n_pos * tm), lambda i: (jnp.minimum(i + step, nt - 1), 0, 0),
                                         memory_space=pltpu.SMEM)
    consts = () if fox is None else _fox_in_consts(fox)
    out_specs = [row(D_MODEL)]
    out_shape = [jax.ShapeDtypeStruct((t, D_MODEL), F32)]
    if fox is not None:
        out_specs += [row(D_MODEL), row(D_MODEL), row(LANES), row(D_MODEL), row(D_MODEL)]
        out_shape += [jax.ShapeDtypeStruct((t, D_MODEL), F32), jax.ShapeDtypeStruct((t, D_MODEL), F32),
                      jax.ShapeDtypeStruct((t, LANES), F32), jax.ShapeDtypeStruct((t, D_MODEL), BF16),
                      jax.ShapeDtypeStruct((t, D_MODEL), BF16)]
    res = pl.pallas_call(
        functools.partial(_combine_kernel, tm=tm, n_pos=n_pos, n_fox_consts=len(consts)),
        grid=(nt,),
        in_specs=[pos_spec(0), pos_spec(1), row(D_MODEL), pl.BlockSpec(memory_space=pl.ANY)]
                 + [_resident(a) for a in consts],
        out_specs=out_specs,
        out_shape=out_shape,
        scratch_shapes=[pltpu.VMEM((2, tm * TILE_ROWS, LANES), F32)] * n_pos + [pltpu.SemaphoreType.DMA((2,))],
        compiler_params=_cparams(("arbitrary",)),
        name="moe_combine" if fox is None else "moe_combine_fox_in",
    )(pos, pos, h, out_slots, *consts)
    return res[0] if fox is None else res


def _pair_tables():
    pairs = [(a, b) for a in range(EXPERTS_PER_GROUP) for b in range(a + 1, EXPERTS_PER_GROUP)]
    lo = [EXPERTS_PER_GROUP * g + a for g in range(N_GROUPS) for a, _ in pairs]
    hi = [EXPERTS_PER_GROUP * g + b for g in range(N_GROUPS) for _, b in pairs]
    return jnp.array(lo, I32), jnp.array(hi, I32)


def _bins_are_pairs(t):
    return t // MOE_BLOCK >= 2 * N_PAIRS


def _moe(h, p, w_gate, w_up, w_down, layer, fox=None):
    t = h.shape[0]
    by_pair = _bins_are_pairs(t)
    n_bins, n_pos, n_exp = (N_PAIRS, 1, 2) if by_pair else (N_EXPERTS, 2, 1)
    xp, rec_t, cnt = _router(h, p["g"], p["w_r_hi"], p["w_r_lo"], p["b_r"], by_pair)

    counts = cnt[0, :n_bins].astype(I32)
    blocks_per_bin = (counts + MOE_BLOCK - 1) // MOE_BLOCK
    block_end = jnp.cumsum(blocks_per_bin)
    slot_base = (block_end - blocks_per_bin) * MOE_BLOCK
    n_blocks = n_pos * t // MOE_BLOCK + n_bins
    n_slots = n_blocks * MOE_BLOCK
    block_bin = jnp.minimum(
        jnp.sum(block_end[None, :] <= jnp.arange(n_blocks, dtype=I32)[:, None], axis=1), n_bins - 1).astype(I32)
    n_used = block_end[-1:].astype(I32)
    bin_ids = jnp.arange(n_bins, dtype=F32)[:, None]
    base_of = lambda bin_row: jnp.sum(jnp.where(bin_row[None, :] == bin_ids, slot_base[:, None], 0), axis=0)
    if by_pair:
        pair_lo, pair_hi = _pair_tables()
        block_experts = jnp.concatenate([pair_lo[block_bin], pair_hi[block_bin]])
        pos_cols = [base_of(rec_t[REC_PAIR]) + rec_t[REC_RANK_A].astype(I32)]
    else:
        block_experts = block_bin
        pos_cols = [base_of(rec_t[REC_EA]) + rec_t[REC_RANK_A].astype(I32),
                    base_of(rec_t[REC_EB]) + rec_t[REC_RANK_B].astype(I32)]

    xs = _dispatch(pos_cols, xp, n_slots)
    out_slots = _experts(block_experts, n_used, xs, w_gate, w_up, w_down, layer, n_exp)
    return _combine(pos_cols, h, out_slots, fox)


def _fox_in_body(x, gkv_ref, gmix_ref, wk_ref, wv_ref, wf_ref, bf_ref, wq_ref, wg_ref,
                 gk_ref, gq_ref, hsum_ref, hexp_ref, k_ref, v_ref, lf_ref, q_ref, sg_ref, *, tm):
    xn = x * _rms_scale(x)
    x_kv = (xn * gkv_ref[...]).astype(BF16)
    x_q = (xn * gmix_ref[...]).astype(BF16)
    def head_norm(y):
        ms = jnp.dot((y * y).astype(BF16), hsum_ref[...], preferred_element_type=F32) * (1.0 / FOX_DH)
        scale = lax.rsqrt(ms + EPS).astype(BF16)
        return y * jnp.dot(scale, hexp_ref[...], preferred_element_type=F32)

    k = jnp.dot(x_kv, wk_ref[...], preferred_element_type=F32)
    k_ref[...] = head_norm(k) * gk_ref[...]
    v_ref[...] = jnp.dot(x_kv, wv_ref[...], preferred_element_type=F32)
    fl = jnp.dot(x_kv, wf_ref[...], preferred_element_type=F32) + bf_ref[...]
    lane = lax.broadcasted_iota(I32, (tm, LANES), 1)
    lf_ref[...] = jnp.where(lane < FOX_HEADS, jax.nn.log_sigmoid(fl), 0.0)
    q = jnp.dot(x_q, wq_ref[...], preferred_element_type=F32)
    q_ref[...] = (head_norm(q) * gq_ref[...]).astype(BF16)
    gate = jnp.dot(x_q, wg_ref[...], preferred_element_type=F32)
    sg_ref[...] = jax.nn.sigmoid(gate).astype(BF16)


def _fox_in_consts(p):
    return (p["g_kv"], p["g_mix"], p["w_k"], p["w_v"], p["w_f"], p["b_f"], p["w_q"], p["w_g"],
            p["g_k"], p["g_q"], p["head_sum"], p["head_expand"])


CUMSUM_ROWS = 128


def _cumsum_kernel(x_ref, c_ref, *, seq):
    r_i = lax.broadcasted_iota(I32, (CUMSUM_ROWS, CUMSUM_ROWS), 0)
    c_i = lax.broadcasted_iota(I32, (CUMSUM_ROWS, CUMSUM_ROWS), 1)
    tri = jnp.where(c_i <= r_i, 1.0, 0.0).astype(BF16)

    def body(g, carry):
        rows = pl.ds(pl.multiple_of(g * CUMSUM_ROWS, CUMSUM_ROWS), CUMSUM_ROWS)
        x = x_ref[0, rows, :]
        x1 = x.astype(BF16)
        rem = x - x1.astype(F32)
        x2 = rem.astype(BF16)
        x3 = (rem - x2.astype(F32)).astype(BF16)
        c = (jnp.dot(tri, x1, preferred_element_type=F32) + jnp.dot(tri, x2, preferred_element_type=F32)
             + jnp.dot(tri, x3, preferred_element_type=F32)) + carry
        c_ref[0, rows, :] = c
        return c[CUMSUM_ROWS - 1:CUMSUM_ROWS, :]

    lax.fori_loop(0, seq // CUMSUM_ROWS, body, jnp.zeros((1, LANES), F32))


def _cumsum(x):
    batch, seq, _ = x.shape
    assert seq % CUMSUM_ROWS == 0
    spec = pl.BlockSpec((1, seq, LANES), lambda b: (b, 0, 0))
    return pl.pallas_call(
        functools.partial(_cumsum_kernel, seq=seq),
        grid=(batch,),
        in_specs=[spec],
        out_specs=spec,
        out_shape=jax.ShapeDtypeStruct(x.shape, F32),
        compiler_params=_cparams(("parallel",)),
        name="logf_cumsum",
    )(x)


LOG2E = 1.4426950408889634
KEY_ALIGN = 128


def _split3(x):
    x1 = x.astype(BF16).astype(F32)
    r = x - x1
    x2 = r.astype(BF16).astype(F32)
    return x1, x2, r - x2


def _fox_attn_kernel(q_ref, k_ref, v_ref, c_ref, sg_ref, o_ref, *, seq_q, tq, n_past, n_keys):
    hp = pl.program_id(1)
    lane_k = lax.broadcasted_iota(I32, (n_keys, LANES), 1)
    lane_q = lax.broadcasted_iota(I32, (seq_q, LANES), 1)
    c_pair = c_ref[0] * LOG2E
    k_pair = k_ref[...]
    v_pair = v_ref[...]
    q_pair = q_ref[...].astype(F32)

    k_aug, v_aug, q_aug, den_lane = [], [], [], []
    for hh in range(2):
        own_lo = hh * FOX_DH
        other = FOX_DH - own_lo
        c_k = jnp.sum(jnp.where(lane_k == 2 * hp + hh, c_pair, 0.0), axis=1, keepdims=True)
        c1, c2, c3 = _split3(c_k)
        d_k = lane_k - other
        extra_k = jnp.where((d_k >= 0) & (d_k < 3), 1.0,
                            jnp.where(d_k == 3, -c1, jnp.where(d_k == 4, -c2, jnp.where(d_k == 5, -c3, 0.0))))
        own_k = (lane_k >= own_lo) & (lane_k < own_lo + FOX_DH)
        k_aug.append(jnp.where(own_k, k_pair, extra_k).astype(BF16))
        v_aug.append(jnp.where(own_k, v_pair, jnp.where(d_k == 0, 1.0, 0.0)).astype(BF16))
        q1, q2, q3 = (c[n_past:n_past + seq_q] for c in (c1, c2, c3))
        d_q = lane_q - other
        extra_q = jnp.where(d_q == 0, q1, jnp.where(d_q == 1, q2, jnp.where(d_q == 2, q3,
                            jnp.where((d_q >= 3) & (d_q < 6), 1.0, 0.0))))
        own_q = (lane_q >= own_lo) & (lane_q < own_lo + FOX_DH)
        q_aug.append(jnp.where(own_q, q_pair, extra_q).astype(BF16))
        den_lane.append(other)

    nt = (((1,), (1,)), ((), ()))
    lane_o = lax.broadcasted_iota(I32, (tq, LANES), 1)
    for r0 in range(0, seq_q, tq):
        first_q = n_past + r0
        n_full = (first_q + 1) // KEY_ALIGN * KEY_ALIGN
        n_vis = min(-(-(first_q + tq) // KEY_ALIGN) * KEY_ALIGN, n_keys)
        k_pos = n_full + lax.broadcasted_iota(I32, (tq, n_vis - n_full), 1)
        q_pos = first_q + lax.broadcasted_iota(I32, (tq, n_vis - n_full), 0)
        visible = k_pos <= q_pos
        heads = []
        for hh in range(2):
            qa = q_aug[hh][r0:r0 + tq]
            s_edge = lax.dot_general(qa, k_aug[hh][n_full:n_vis], nt, preferred_element_type=F32)
            s_edge = jnp.where(visible, s_edge, -jnp.inf)
            m = jnp.max(s_edge, axis=1, keepdims=True)
            if n_full:
                s_full = lax.dot_general(qa, k_aug[hh][:n_full], nt, preferred_element_type=F32)
                m = jnp.maximum(m, jnp.max(s_full, axis=1, keepdims=True))
            acc = jnp.dot(jnp.exp2(s_edge - m).astype(BF16), v_aug[hh][n_full:n_vis],
                          preferred_element_type=F32)
            if n_full:
                acc = acc + jnp.dot(jnp.exp2(s_full - m).astype(BF16), v_aug[hh][:n_full],
                                    preferred_element_type=F32)
            denom = jnp.sum(jnp.where(lane_o == den_lane[hh], acc, 0.0), axis=1, keepdims=True)
            heads.append(acc / denom)
        o = jnp.where(lane_o < FOX_DH, heads[0], heads[1])
        o_ref[r0:r0 + tq, :] = (o * sg_ref[r0:r0 + tq, :].astype(F32)).astype(BF16)


def _fox_attention(q, sg, k_all, v_all, c_all, batch, seq_q, n_keys, n_past, tq):
    assert seq_q % tq == 0 and n_keys % KEY_ALIGN == 0
    kern = functools.partial(_fox_attn_kernel, seq_q=seq_q, tq=tq, n_past=n_past, n_keys=n_keys)
    return pl.pallas_call(
        kern,
        grid=(batch, FOX_PAIRS),
        in_specs=[pl.BlockSpec((seq_q, LANES), lambda b, hp: (b, hp)),
                  pl.BlockSpec((n_keys, LANES), lambda b, hp: (b, hp)),
                  pl.BlockSpec((n_keys, LANES), lambda b, hp: (b, hp)),
                  pl.BlockSpec((1, n_keys, LANES), lambda b, hp: (b, 0, 0)),
                  pl.BlockSpec((seq_q, LANES), lambda b, hp: (b, hp))],
        out_specs=pl.BlockSpec((seq_q, LANES), lambda b, hp: (b, hp)),
        out_shape=jax.ShapeDtypeStruct((batch * seq_q, D_MODEL), BF16),
        compiler_params=_cparams(("parallel", "parallel")),
        name="fox_attention",
    )(q, k_all, v_all, c_all, sg)


def _pad_cols(w, n):
    return jnp.pad(w, ((0, 0), (0, n - w.shape[1])))


def _prepare(norm_mix, norm_ffn, w_gla_in, w_gla_gate2, b_gla_gate, g_gla_head, w_gla_out, g_kv, w_kv,
             b_forget, g_k, w_fox_qg, g_q, w_fox_out, w_group, b_group, w_router, b_router):
    row = lambda a: a.reshape(1, -1).astype(F32)
    w_in = w_gla_in[0]
    c0, c1, c2, c3 = 2 * GLA_QK, 2 * GLA_QK + GLA_V, 2 * GLA_QK + GLA_V + GLA_GATE_RANK, w_in.shape[1]
    gla = dict(
        g=row(norm_mix[0]),
        w_qk=w_in[:, :c0].astype(BF16), w_v=w_in[:, c0:c1].astype(BF16), w_r=w_in[:, c2:c3].astype(BF16),
        w_g=_pad_cols(w_in[:, c1:c2], LANES).astype(BF16),
        w_g2=jnp.pad(w_gla_gate2[0], ((0, LANES - GLA_GATE_RANK), (0, 0))).astype(BF16),
        b_g=row(b_gla_gate[0]), g_head=row(g_gla_head[0]), w_out=w_gla_out[0].astype(BF16),
    )
    head_id = jnp.arange(D_MODEL) // FOX_DH
    fox = dict(
        g_kv=row(g_kv), g_mix=row(norm_mix[1]),
        w_k=w_kv[:, :D_MODEL].astype(BF16), w_v=w_kv[:, D_MODEL:2 * D_MODEL].astype(BF16),
        w_f=_pad_cols(w_kv[:, 2 * D_MODEL:], LANES).astype(BF16),
        b_f=_pad_cols(row(b_forget), LANES),
        w_q=w_fox_qg[0][:, :D_MODEL].astype(BF16), w_g=w_fox_qg[0][:, D_MODEL:].astype(BF16),
        g_k=row(jnp.tile(g_k, FOX_HEADS)), g_q=row(jnp.tile(g_q[0], FOX_HEADS)) * (FOX_DH ** -0.5 * LOG2E),
        head_sum=(head_id[:, None] == jnp.arange(LANES)[None, :]).astype(BF16),
        head_expand=(jnp.arange(LANES)[:, None] == head_id[None, :]).astype(BF16),
        w_out=w_fox_out[0].astype(BF16),
    )
    moe = []
    for layer in range(2):
        w_r = _pad_cols(jnp.concatenate([w_group[layer], w_router[layer]], axis=1), LANES)
        w_r_hi = w_r.astype(BF16)
        moe.append(dict(
            g=row(norm_ffn[layer]),
            w_r_hi=w_r_hi, w_r_lo=(w_r - w_r_hi.astype(F32)).astype(BF16),
            b_r=_pad_cols(jnp.concatenate([row(b_group[layer]), row(b_router[layer])], axis=1), LANES),
        ))
    return gla, fox, moe


def _trunk(x, s0, past, gla, fox, moe, experts):
    batch, seq, _ = x.shape
    t = batch * seq
    h = x.reshape(t, D_MODEL)

    q, k, v, r, la = _gla_in_proj(h, gla["g"], gla["w_qk"], gla["w_v"], gla["w_r"], gla["w_g"], gla["w_g2"],
                                  gla["b_g"])
    o, s_new = _gla(q, k, v, la, s0, gla["g_head"], batch, seq)
    h = _out_proj(o, r, gla["w_out"], h)
    h, k_new, v_new, lf_new, qf, sg = _moe(h, moe[0], *experts, 0, fox)
    lf3 = lf_new.reshape(batch, seq, LANES)
    if past is None:
        n_past, n_keys = 0, seq
        tq = min(seq, 512)
        k_all, v_all, lf_all = k_new, v_new, lf3
    else:
        past_k, past_v, past_lf = past
        n_past = past_k.shape[1]
        tq = seq
        n_keys = -(-(n_past + seq) // KEY_ALIGN) * KEY_ALIGN
        pad = n_keys - n_past - seq
        cat = lambda a, b: jnp.concatenate(
            [a, b, jnp.zeros((batch, pad, a.shape[2]), F32)], axis=1)
        k_all = cat(past_k.reshape(batch, n_past, D_MODEL), k_new.reshape(batch, seq, D_MODEL)
                    ).reshape(batch * n_keys, D_MODEL)
        v_all = cat(past_v.reshape(batch, n_past, D_MODEL), v_new.reshape(batch, seq, D_MODEL)
                    ).reshape(batch * n_keys, D_MODEL)
        lf_all = cat(jnp.pad(past_lf, ((0, 0), (0, 0), (0, LANES - FOX_HEADS))), lf3)
    c_all = _cumsum(lf_all)
    o = _fox_attention(qf, sg, k_all, v_all, c_all, batch, seq, n_keys, n_past, tq)
    h = _out_proj(o, None, fox["w_out"], h)
    h = _moe(h, moe[1], *experts, 1)

    return (h.reshape(batch, seq, D_MODEL),
            k_new.reshape(batch, seq, FOX_HEADS, FOX_DH),
            v_new.reshape(batch, seq, FOX_HEADS, FOX_DH),
            lf3[:, :, :FOX_HEADS],
            s_new[None])


def kernel(x_prompt, x_sample, cache_k, cache_v, cache_logf, state_gla, norm_mix, norm_ffn, w_gla_in, w_gla_gate2, b_gla_gate, g_gla_head, w_gla_out, g_kv, w_kv, b_forget, g_k, w_fox_qg, g_q, w_fox_out, w_group, b_group, w_router, b_router, w_exp_gate, w_exp_up, w_exp_down):
    gla, fox, moe = _prepare(norm_mix, norm_ffn, w_gla_in, w_gla_gate2, b_gla_gate, g_gla_head, w_gla_out, g_kv,
                             w_kv, b_forget, g_k, w_fox_qg, g_q, w_fox_out, w_group, b_group, w_router, b_router)
    experts = (w_exp_gate.astype(F32), w_exp_up.astype(F32), w_exp_down.astype(F32))
    s_zero = jnp.zeros((x_prompt.shape[0], GLA_HEADS, GLA_DK, GLA_DV), F32)
    y_p, k_p, v_p, lf_p, s_p = _trunk(x_prompt, s_zero, None, gla, fox, moe, experts)
    y_s, k_s, v_s, lf_s, s_s = _trunk(x_sample, state_gla[0].astype(F32),
                                      (cache_k.astype(F32), cache_v.astype(F32), cache_logf.astype(F32)),
                                      gla, fox, moe, experts)
    return (y_p, y_s, k_p, v_p, lf_p, s_p, k_s, v_s, lf_s, s_s)
```

```python
import functools

import jax
import jax.numpy as jnp
from jax import lax
from jax.experimental import pallas as pl
from jax.experimental.pallas import tpu as pltpu

F32 = jnp.float32
BF16 = jnp.bfloat16
U32 = jnp.uint32
I32 = jnp.int32

EPS = 1e-6
D_MODEL = 1024
HALF_D = D_MODEL // 2
LANES = 128
VMEM_LIMIT_BYTES = 56 * 1024 * 1024

GLA_HEADS = 4
GLA_DK = 128
GLA_DV = 256
GLA_QK = GLA_HEADS * GLA_DK
GLA_V = GLA_HEADS * GLA_DV
GLA_GATE_RANK = 16
GLA_GATE_TAU = 16.0
GLA_CHUNK = 64
GLA_GROUP = 256

FOX_HEADS = 16
FOX_DH = 64
FOX_PAIRS = FOX_HEADS // 2

N_GROUPS = 4
EXPERTS_PER_GROUP = 8
N_EXPERTS = N_GROUPS * EXPERTS_PER_GROUP
D_EXPERT = 512
MOE_BLOCK = 256
ROW_DMA_UNROLL = 16


def _cparams(sem):
    return pltpu.CompilerParams(dimension_semantics=sem, vmem_limit_bytes=VMEM_LIMIT_BYTES)


def _row_tile(t, pref):
    tm = min(t, pref)
    assert t % tm == 0
    return tm


def _resident(a):
    return pl.BlockSpec(a.shape, lambda *_: (0,) * a.ndim, pipeline_mode=pl.Buffered(1))


def _rms_scale(x):
    return lax.rsqrt(jnp.mean(x * x, axis=-1, keepdims=True) + EPS)


def _pack_halves(y):
    lo = lax.bitcast_convert_type(y[:, :HALF_D].astype(BF16).astype(F32), U32)
    hi = lax.bitcast_convert_type(y[:, HALF_D:].astype(BF16).astype(F32), U32)
    return (lo >> 16) | (hi & jnp.uint32(0xFFFF0000))


def _unpack_halves(p):
    lo = lax.bitcast_convert_type(p << 16, F32)
    hi = lax.bitcast_convert_type(p & jnp.uint32(0xFFFF0000), F32)
    return lo, hi


TILE_ROWS = 8
ROW_PIECES = D_MODEL // LANES
assert ROW_PIECES == TILE_ROWS


def _load_tiled_rows(ref, n, pieces):
    return jnp.concatenate([ref[pl.ds(j, n, stride=TILE_ROWS), :] for j in pieces], axis=1)


def _store_tiled_rows(ref, val):
    n = val.shape[0]
    for j in range(ROW_PIECES):
        ref[pl.ds(j, n, stride=TILE_ROWS), :] = val[:, j * LANES:(j + 1) * LANES]


def _gla_in_kernel(h_ref, g_ref, wqk_ref, wv_ref, wr_ref, wg_ref, wg2_ref, bg_ref,
                   q_ref, k_ref, v_ref, r_ref, la_ref):
    x = h_ref[...]
    hn = (x * _rms_scale(x) * g_ref[...]).astype(BF16)
    qk = jnp.dot(hn, wqk_ref[...], preferred_element_type=F32)
    q_ref[...] = (qk[:, :GLA_QK] * (GLA_DK ** -0.5)).astype(BF16)
    k_ref[...] = qk[:, GLA_QK:].astype(BF16)
    v_ref[...] = jnp.dot(hn, wv_ref[...], preferred_element_type=F32).astype(BF16)
    r_ref[...] = jnp.dot(hn, wr_ref[...], preferred_element_type=F32).astype(BF16)
    g_lr = jnp.dot(hn, wg_ref[...], preferred_element_type=F32)
    z = jnp.dot(g_lr.astype(BF16), wg2_ref[...], preferred_element_type=F32) + bg_ref[...]
    la_ref[...] = jax.nn.log_sigmoid(z) / GLA_GATE_TAU


def _gla_in_proj(h, g, wqk, wv, wr, wg, wg2, bg):
    t = h.shape[0]
    tm = _row_tile(t, 512)
    row = lambda n: pl.BlockSpec((tm, n), lambda i: (i, 0))
    full = _resident
    return pl.pallas_call(
        _gla_in_kernel,
        grid=(t // tm,),
        in_specs=[row(D_MODEL), full(g), full(wqk), full(wv), full(wr), full(wg), full(wg2), full(bg)],
        out_specs=[row(GLA_QK), row(GLA_QK), row(GLA_V), row(GLA_V), row(GLA_QK)],
        out_shape=[jax.ShapeDtypeStruct((t, GLA_QK), BF16), jax.ShapeDtypeStruct((t, GLA_QK), BF16),
                   jax.ShapeDtypeStruct((t, GLA_V), BF16), jax.ShapeDtypeStruct((t, GLA_V), BF16),
                   jax.ShapeDtypeStruct((t, GLA_QK), F32)],
        compiler_params=_cparams(("parallel",)),
        name="gla_in_proj",
    )(h, g, wqk, wv, wr, wg, wg2, bg)


def _gla_kernel(q_ref, k_ref, v_ref, la_ref, s0_ref, gh_ref, o_ref, s_ref,
                qb_s, oi_s, u_s, dl_s, *, seq, chunk, group):
    n_groups = seq // group
    n_chunks = seq // chunk
    per_group = group // chunk

    shift = chunk.bit_length() - 1
    assert chunk == 1 << shift and group & (group - 1) == 0
    row = lax.broadcasted_iota(I32, (group, group), 0)
    col = lax.broadcasted_iota(I32, (group, group), 1)
    tril = ((row >> shift) == (col >> shift)) & (col <= row)
    tril_bf = jnp.where(tril, 1.0, 0.0).astype(BF16)

    def chunk_rows(b, at):
        return jnp.concatenate(
            [jnp.broadcast_to(b[j * chunk + at:j * chunk + at + 1], (chunk, GLA_DK)) for j in range(per_group)],
            axis=0)

    def group_body(g, carry):
        r0 = pl.multiple_of(g * group, group)
        rows = pl.ds(r0, group)
        la = la_ref[rows, :]
        la_hi = la.astype(BF16)
        la_lo = (la - la_hi.astype(F32)).astype(BF16)
        b = (jnp.dot(tril_bf, la_hi, preferred_element_type=F32)
             + jnp.dot(tril_bf, la_lo, preferred_element_type=F32))
        b_ref = chunk_rows(b, chunk // 2)
        b_last = chunk_rows(b, chunk - 1)
        qf = q_ref[rows, :].astype(F32)
        kf = k_ref[rows, :].astype(F32)
        vg = v_ref[rows, :]
        qe = (qf * jnp.exp(b - b_ref)).astype(BF16)
        ke = (kf * jnp.exp(b_ref - b)).astype(BF16)
        att = lax.dot_general(qe, ke, (((1,), (1,)), ((), ())), preferred_element_type=F32)
        att = jnp.where(tril, att, 0.0).astype(BF16)
        oi_s[rows, :] = jnp.dot(att, vg, preferred_element_type=F32)
        qb_s[rows, :] = (qf * jnp.exp(b)).astype(BF16)
        kl = (kf * jnp.exp(b_last - b)).astype(BF16)
        decay = jnp.exp(b_last)
        for j in range(per_group):
            c = g * per_group + j
            sl = slice(j * chunk, (j + 1) * chunk)
            u_s[c] = lax.dot_general(vg[sl], kl[sl], (((0,), (0,)), ((), ())), preferred_element_type=F32)
            dl_s[c] = decay[j * chunk:j * chunk + 1]
        return carry

    lax.fori_loop(0, n_groups, group_body, 0, unroll=min(2, n_groups))

    gh = gh_ref[...]

    def chunk_body(c, st):
        r0 = pl.multiple_of(c * chunk, chunk)
        rows = pl.ds(r0, chunk)
        o = oi_s[rows, :] + lax.dot_general(qb_s[rows, :], st.astype(BF16), (((1,), (1,)), ((), ())),
                                            preferred_element_type=F32)
        o_ref[rows, :] = (o * _rms_scale(o) * gh).astype(BF16)
        return st * dl_s[c] + u_s[c]

    st = lax.fori_loop(0, n_chunks, chunk_body, s0_ref[0, 0].T, unroll=min(4, n_chunks))
    s_ref[0, 0] = st.T


def _gla(q, k, v, la, s0, g_head, batch, seq):
    chunk = min(seq, GLA_CHUNK)
    group = min(seq, GLA_GROUP)
    assert seq % group == 0 and group % chunk == 0
    n_chunks = seq // chunk
    kern = functools.partial(_gla_kernel, seq=seq, chunk=chunk, group=group)
    return pl.pallas_call(
        kern,
        grid=(batch, GLA_HEADS),
        in_specs=[pl.BlockSpec((seq, GLA_DK), lambda b, h: (b, h)),
                  pl.BlockSpec((seq, GLA_DK), lambda b, h: (b, h)),
                  pl.BlockSpec((seq, GLA_DV), lambda b, h: (b, h)),
                  pl.BlockSpec((seq, GLA_DK), lambda b, h: (b, h)),
                  pl.BlockSpec((1, 1, GLA_DK, GLA_DV), lambda b, h: (b, h, 0, 0)),
                  pl.BlockSpec((1, GLA_DV), lambda b, h: (0, 0))],
        out_specs=[pl.BlockSpec((seq, GLA_DV), lambda b, h: (b, h)),
                   pl.BlockSpec((1, 1, GLA_DK, GLA_DV), lambda b, h: (b, h, 0, 0))],
        out_shape=[jax.ShapeDtypeStruct((batch * seq, GLA_V), BF16),
                   jax.ShapeDtypeStruct((batch, GLA_HEADS, GLA_DK, GLA_DV), F32)],
        scratch_shapes=[pltpu.VMEM((seq, GLA_DK), BF16),
                        pltpu.VMEM((seq, GLA_DV), F32),
                        pltpu.VMEM((n_chunks, GLA_DV, GLA_DK), F32),
                        pltpu.VMEM((n_chunks, 1, GLA_DK), F32)],
        compiler_params=_cparams(("parallel", "parallel")),
        name="gla_scan",
    )(q, k, v, la, s0, g_head)


def _out_proj_kernel(*refs, gated, tm, by_pair):
    if gated:
        a_ref, r_ref = refs[:2]
        rest = refs[2:]
        r = r_ref[...].astype(F32)
        a = (a_ref[...].astype(F32) * (r * jax.nn.sigmoid(r))).astype(BF16)
    else:
        rest = refs[1:]
        a = refs[0][...]
    w_ref, h_ref = rest[:2]
    router_refs = rest[2:6]
    o_ref = rest[6]
    y = h_ref[...] + jnp.dot(a, w_ref[...], preferred_element_type=F32)
    o_ref[...] = y
    _router_body(y, *router_refs, *rest[7:], tm=tm, by_pair=by_pair)


def _out_proj(a, r, w, h, moe_p, by_pair):
    t = h.shape[0]
    tm = _row_tile(t, 512)
    row = pl.BlockSpec((tm, D_MODEL), lambda i: (i, 0))
    gated = r is not None
    router_consts = (moe_p["g"], moe_p["w_r_hi"], moe_p["w_r_lo"], moe_p["b_r"])
    args = ((a, r) if gated else (a,)) + (w, h) + router_consts
    specs = ([row, row] if gated else [row]) + [_resident(w), row] + [_resident(c) for c in router_consts]
    return pl.pallas_call(
        functools.partial(_out_proj_kernel, gated=gated, tm=tm, by_pair=by_pair),
        grid=(t // tm,),
        in_specs=specs,
        out_specs=[row,
                   pl.BlockSpec((tm * TILE_ROWS, LANES), lambda i: (i, 0)),
                   pl.BlockSpec((REC_WIDTH, tm), lambda i: (0, i)),
                   pl.BlockSpec((LANES, 1), lambda i: (0, 0))],
        out_shape=[jax.ShapeDtypeStruct((t, D_MODEL), F32),
                   jax.ShapeDtypeStruct((t * TILE_ROWS, LANES), U32),
                   jax.ShapeDtypeStruct((REC_WIDTH, t), F32),
                   jax.ShapeDtypeStruct((LANES, 1), F32)],
        scratch_shapes=[pltpu.VMEM((LANES, 1), F32)],
        compiler_params=_cparams(("arbitrary",)),
        name="out_proj_gated_router" if gated else "out_proj_router",
    )(*args)


REC_EA, REC_EB, REC_PAIR, REC_RANK_A, REC_RANK_B = range(5)
REC_WIDTH = 8
META_WA, META_WB, META_EA, META_EB = range(4)
META_PIECE = HALF_D // LANES
IN_PIECES = range(META_PIECE + 1)
LOGIT_ROWS = -(-(N_GROUPS + N_EXPERTS) // TILE_ROWS) * TILE_ROWS
PAIRS_PER_GROUP = EXPERTS_PER_GROUP * (EXPERTS_PER_GROUP - 1) // 2
N_PAIRS = N_GROUPS * PAIRS_PER_GROUP


def _router_body(x, g_ref, wt_hi_ref, wt_lo_ref, bias_ref, xp_ref, rec_t_ref, cnt_ref, carry_s, *, tm, by_pair):
    @pl.when(pl.program_id(0) == 0)
    def _():
        carry_s[...] = jnp.zeros_like(carry_s)

    hn = x * _rms_scale(x) * g_ref[...]

    hi = hn.astype(BF16)
    lo = (hn - hi.astype(F32)).astype(BF16)
    nt = (((1,), (1,)), ((), ()))
    wt_hi = wt_hi_ref[...]
    lg_all = (lax.dot_general(wt_hi, hi, nt, preferred_element_type=F32)
              + lax.dot_general(wt_hi, lo, nt, preferred_element_type=F32)
              + lax.dot_general(wt_lo_ref[...], hi, nt, preferred_element_type=F32) + bias_ref[...])
    lg = lg_all[:LOGIT_ROWS]

    sub = lax.broadcasted_iota(I32, (LOGIT_ROWS, tm), 0).astype(F32)
    neg = jnp.float32(-jnp.inf)

    def masked_softmax(mask):
        m = jnp.max(jnp.where(mask, lg, neg), axis=0, keepdims=True)
        e = jnp.where(mask, jnp.exp(lg - m), 0.0)
        return e / jnp.sum(e, axis=0, keepdims=True)

    def top1(p, mask):
        v = jnp.max(jnp.where(mask, p, -1.0), axis=0, keepdims=True)
        idx = jnp.min(jnp.where(mask & (p == v), sub, float(LANES)), axis=0, keepdims=True)
        return v, idx

    gmask = sub < N_GROUPS
    g_top, g_idx = top1(masked_softmax(gmask), gmask)
    e_lo = N_GROUPS + EXPERTS_PER_GROUP * g_idx
    emask = (sub >= e_lo) & (sub < e_lo + EXPERTS_PER_GROUP)
    ep = masked_softmax(emask)
    p1, i1 = top1(ep, emask)
    mask2 = emask & (sub != i1)
    p2, i2 = top1(ep, mask2)
    denom = p1 + p2
    w0 = g_top * p1 / denom
    w1 = g_top * p2 / denom

    a0 = i1 - e_lo
    a1 = i2 - e_lo
    first_low = a0 < a1
    lo_l = jnp.where(first_low, a0, a1)
    hi_l = jnp.where(first_low, a1, a0)
    w_a = jnp.where(first_low, w0, w1)
    w_b = jnp.where(first_low, w1, w0)
    e_a = EXPERTS_PER_GROUP * g_idx + lo_l
    e_b = EXPERTS_PER_GROUP * g_idx + hi_l
    pair = (PAIRS_PER_GROUP * g_idx + lo_l * (2 * EXPERTS_PER_GROUP - 1 - lo_l) * 0.5 + (hi_l - lo_l - 1.0))

    def stack_rows(vals):
        row8 = lax.broadcasted_iota(I32, (TILE_ROWS, tm), 0)
        out = jnp.zeros((TILE_ROWS, tm), F32)
        for k, val in enumerate(vals):
            out = jnp.where(row8 == k, val, out)
        return out

    meta_t = jnp.concatenate([stack_rows([w_a, w_b, e_a, e_b]), jnp.zeros((LANES - TILE_ROWS, tm), F32)], axis=0)
    _store_tiled_rows(xp_ref, jnp.concatenate(
        [_pack_halves(hn), lax.bitcast_convert_type(meta_t.T, U32),
         jnp.zeros((tm, D_MODEL - HALF_D - LANES), U32)], axis=1))

    bins = lax.broadcasted_iota(I32, (LANES, tm), 0).astype(F32)
    if by_pair:
        hit_a = bins == pair
        onehot = jnp.where(hit_a, 1.0, 0.0)
    else:
        hit_a = bins == e_a
        hit_b = bins == e_b
        onehot = jnp.where(hit_a | hit_b, 1.0, 0.0)
    r_i = lax.broadcasted_iota(I32, (tm, tm), 0)
    c_i = lax.broadcasted_iota(I32, (tm, tm), 1)
    earlier = jnp.where(r_i < c_i, 1.0, 0.0).astype(BF16)
    prior = jnp.dot(onehot.astype(BF16), earlier, preferred_element_type=F32) + carry_s[...]
    rank_a = jnp.sum(jnp.where(hit_a, prior, 0.0), axis=0, keepdims=True)
    rank_b = jnp.zeros_like(rank_a) if by_pair else jnp.sum(jnp.where(hit_b, prior, 0.0), axis=0, keepdims=True)
    carry_s[...] = carry_s[...] + jnp.sum(onehot, axis=1, keepdims=True)
    cnt_ref[...] = carry_s[...]

    rec_t_ref[...] = stack_rows([e_a, e_b, pair, rank_a, rank_b])


def _row_copy(src, src_row, dst, dst_row, sem):
    tile = lambda ref, row: ref.at[pl.ds(pl.multiple_of(row * TILE_ROWS, TILE_ROWS), TILE_ROWS)]
    return pltpu.make_async_copy(tile(src, src_row), tile(dst, dst_row), sem)


def _for_each_row(tm, n_pos, fn):
    rows_per_trip = ROW_DMA_UNROLL // n_pos

    def trip(i, carry):
        for j in range(rows_per_trip):
            for k in range(n_pos):
                fn(i * rows_per_trip + j, k, j * n_pos + k)
        return carry

    lax.fori_loop(0, tm // rows_per_trip, trip, 0)


def _dispatch_kernel(pos_ref, xp_ref, zeros_hbm, xs_hbm, sem, *, tm, n_pos):
    del zeros_hbm
    _for_each_row(tm, n_pos, lambda r, k, n: _row_copy(
        xp_ref, r, xs_hbm, pos_ref[0, 0, k * tm + r], sem.at[0]).start(priority=n % 2))
    _for_each_row(tm, n_pos, lambda r, k, n: _row_copy(xp_ref, 0, xs_hbm, 0, sem.at[0]).wait())


def _dispatch(pos_cols, xp, n_slots):
    t = xp.shape[0] // TILE_ROWS
    tm = _row_tile(t, 256)
    n_pos = len(pos_cols)
    pos = jnp.concatenate([p.reshape(t // tm, 1, tm) for p in pos_cols], axis=2)
    return pl.pallas_call(
        functools.partial(_dispatch_kernel, tm=tm, n_pos=n_pos),
        grid=(t // tm,),
        in_specs=[pl.BlockSpec((1, 1, n_pos * tm), lambda i: (i, 0, 0), memory_space=pltpu.SMEM),
                  pl.BlockSpec((tm * TILE_ROWS, LANES), lambda i: (i, 0)),
                  pl.BlockSpec(memory_space=pl.ANY)],
        out_specs=pl.BlockSpec(memory_space=pl.ANY),
        out_shape=jax.ShapeDtypeStruct((n_slots * TILE_ROWS, LANES), U32),
        input_output_aliases={2: 0},
        scratch_shapes=[pltpu.SemaphoreType.DMA((1,))],
        compiler_params=_cparams(("arbitrary",)),
        name="moe_dispatch",
    )(pos, xp, jnp.zeros((n_slots * TILE_ROWS, LANES), U32))


def _expert_kernel(*refs, n_exp, n_blocks):
    be_ref, nb_ref, xs_ref = refs[:3]
    w_refs = refs[3:3 + 3 * n_exp]
    out_ref = refs[3 + 3 * n_exp]
    scratch = refs[4 + 3 * n_exp:]
    b = pl.program_id(0)

    @pl.when(b < nb_ref[0])
    def _():
        prev = jnp.maximum(b - 1, 0)
        for j in range(n_exp):
            wg_ref, wu_ref, wd_ref = w_refs[3 * j:3 * j + 3]
            wgu_s, wd_s = scratch[2 * j:2 * j + 2]

            @pl.when((b == 0) | (be_ref[j * n_blocks + b] != be_ref[j * n_blocks + prev]))
            def _():
                wgu_s[:, :D_EXPERT] = wg_ref[0, 0].astype(BF16)
                wgu_s[:, D_EXPERT:] = wu_ref[0, 0].astype(BF16)
                wd_s[...] = wd_ref[0, 0].astype(BF16)

        rows = _load_tiled_rows(xs_ref, MOE_BLOCK, IN_PIECES)
        lo, hi = _unpack_halves(rows[:, :HALF_D])
        x_lo = lo.astype(BF16)
        x_hi = hi.astype(BF16)
        meta = lax.bitcast_convert_type(rows[:, HALF_D:], F32)
        y = None
        for j in range(n_exp):
            wgu_s, wd_s = scratch[2 * j:2 * j + 2]
            gu = (jnp.dot(x_lo, wgu_s[:HALF_D, :], preferred_element_type=F32)
                  + jnp.dot(x_hi, wgu_s[HALF_D:, :], preferred_element_type=F32))
            gate = gu[:, :D_EXPERT]
            if n_exp == 2:
                w = meta[:, META_WA + j:META_WA + j + 1]
            else:
                this = be_ref[b].astype(F32)
                w = jnp.where(meta[:, META_EA:META_EA + 1] == this,
                              meta[:, META_WA:META_WA + 1], meta[:, META_WB:META_WB + 1])
            hid = (w * (gate * jax.nn.sigmoid(gate) * gu[:, D_EXPERT:])).astype(BF16)
            y_j = jnp.dot(hid, wd_s[...], preferred_element_type=F32)
            y = y_j if y is None else y + y_j
        _store_tiled_rows(out_ref, y)

    @pl.when(b >= nb_ref[0])
    def _():
        out_ref[...] = jnp.zeros_like(out_ref)


def _experts(block_experts, n_blocks_used, xs, w_gate, w_up, w_down, layer, n_exp):
    n_slots = xs.shape[0] // TILE_ROWS
    n_blocks = n_slots // MOE_BLOCK
    block_rows = MOE_BLOCK * TILE_ROWS

    def used(b, nb):
        return jnp.minimum(b, nb[0] - 1)

    def w_spec(shape, j):
        return pl.BlockSpec((1, 1) + shape, lambda b, be, nb: (layer, be[j * n_blocks + used(b, nb)], 0, 0))

    w_specs, w_args, scratch = [], [], []
    for j in range(n_exp):
        w_specs += [w_spec((D_MODEL, D_EXPERT), j), w_spec((D_MODEL, D_EXPERT), j), w_spec((D_EXPERT, D_MODEL), j)]
        w_args += [w_gate, w_up, w_down]
        scratch += [pltpu.VMEM((D_MODEL, 2 * D_EXPERT), BF16), pltpu.VMEM((D_EXPERT, D_MODEL), BF16)]
    grid_spec = pltpu.PrefetchScalarGridSpec(
        num_scalar_prefetch=2,
        grid=(n_blocks,),
        in_specs=[pl.BlockSpec((block_rows, LANES), lambda b, be, nb: (used(b, nb), 0))] + w_specs,
        out_specs=pl.BlockSpec((block_rows, LANES), lambda b, be, nb: (b, 0)),
        scratch_shapes=scratch,
    )
    return pl.pallas_call(
        functools.partial(_expert_kernel, n_exp=n_exp, n_blocks=n_blocks),
        grid_spec=grid_spec,
        out_shape=jax.ShapeDtypeStruct((n_slots * TILE_ROWS, LANES), F32),
        compiler_params=_cparams(("arbitrary",)),
        name="moe_experts",
    )(block_experts, n_blocks_used, xs, *w_args)


def _combine_kernel(*refs, tm, n_pos, n_fox_consts):
    pos_ref, pos_next_ref, h_ref, out_hbm = refs[:4]
    fox_consts = refs[4:4 + n_fox_consts]
    n_out = 6 if n_fox_consts else 1
    outs = refs[4 + n_fox_consts:4 + n_fox_consts + n_out]
    gathered = refs[4 + n_fox_consts + n_out:-1]
    sem = refs[-1]
    i = pl.program_id(0)
    slot = i % 2

    def issue(p_ref, s):
        _for_each_row(tm, n_pos, lambda r, k, n: _row_copy(
            out_hbm, p_ref[0, 0, k * tm + r], gathered[k].at[s], r, sem.at[s]).start(priority=n % 2))

    @pl.when(i == 0)
    def _():
        issue(pos_ref, 0)

    @pl.when(i + 1 < pl.num_programs(0))
    def _():
        issue(pos_next_ref, 1 - slot)

    _for_each_row(tm, n_pos, lambda r, k, n: _row_copy(
        out_hbm, 0, gathered[k].at[slot], 0, sem.at[slot]).wait())

    y = h_ref[...]
    for g_s in gathered:
        y = y + _load_tiled_rows(g_s.at[slot], tm, range(ROW_PIECES))
    outs[0][...] = y
    if n_fox_consts:
        _fox_in_body(y, *fox_consts, *outs[1:], tm=tm)


def _combine(pos_cols, h, out_slots, fox=None):
    t = h.shape[0]
    tm = _row_tile(t, 512)
    nt = t // tm
    n_pos = len(pos_cols)
    pos = jnp.concatenate([p.reshape(nt, 1, tm) for p in pos_cols], axis=2)
    row = lambda n: pl.BlockSpec((tm, n), lambda i: (i, 0))
    pos_spec = lambda step: pl.BlockSpec((1, 1, n_pos * tm), lambda i: (jnp.minimum(i + step, nt - 1), 0, 0),
                                         memory_space=pltpu.SMEM)
    consts = () if fox is None else _fox_in_consts(fox)
    out_specs = [row(D_MODEL)]
    out_shape = [jax.ShapeDtypeStruct((t, D_MODEL), F32)]
    if fox is not None:
        out_specs += [row(D_MODEL), row(D_MODEL), row(LANES), row(D_MODEL), row(D_MODEL)]
        out_shape += [jax.ShapeDtypeStruct((t, D_MODEL), F32), jax.ShapeDtypeStruct((t, D_MODEL), F32),
                      jax.ShapeDtypeStruct((t, LANES), F32), jax.ShapeDtypeStruct((t, D_MODEL), BF16),
                      jax.ShapeDtypeStruct((t, D_MODEL), BF16)]
    res = pl.pallas_call(
        functools.partial(_combine_kernel, tm=tm, n_pos=n_pos, n_fox_consts=len(consts)),
        grid=(nt,),
        in_specs=[pos_spec(0), pos_spec(1), row(D_MODEL), pl.BlockSpec(memory_space=pl.ANY)]
                 + [_resident(a) for a in consts],
        out_specs=out_specs,
        out_shape=out_shape,
        scratch_shapes=[pltpu.VMEM((2, tm * TILE_ROWS, LANES), F32)] * n_pos + [pltpu.SemaphoreType.DMA((2,))],
        compiler_params=_cparams(("arbitrary",)),
        name="moe_combine" if fox is None else "moe_combine_fox_in",
    )(pos, pos, h, out_slots, *consts)
    return res[0] if fox is None else res


def _pair_tables():
    pairs = [(a, b) for a in range(EXPERTS_PER_GROUP) for b in range(a + 1, EXPERTS_PER_GROUP)]
    lo = [EXPERTS_PER_GROUP * g + a for g in range(N_GROUPS) for a, _ in pairs]
    hi = [EXPERTS_PER_GROUP * g + b for g in range(N_GROUPS) for _, b in pairs]
    return jnp.array(lo, I32), jnp.array(hi, I32)


def _bins_are_pairs(t):
    return t // MOE_BLOCK >= 2 * N_PAIRS


def _moe(routed, w_gate, w_up, w_down, layer, fox=None):
    h, xp, rec_t, cnt = routed
    t = h.shape[0]
    by_pair = _bins_are_pairs(t)
    n_bins, n_pos, n_exp = (N_PAIRS, 1, 2) if by_pair else (N_EXPERTS, 2, 1)

    counts = cnt[:n_bins, 0].astype(I32)
    blocks_per_bin = (counts + MOE_BLOCK - 1) // MOE_BLOCK
    block_end = jnp.cumsum(blocks_per_bin)
    slot_base = (block_end - blocks_per_bin) * MOE_BLOCK
    n_blocks = n_pos * t // MOE_BLOCK + n_bins
    n_slots = n_blocks * MOE_BLOCK
    block_bin = jnp.minimum(
        jnp.sum(block_end[None, :] <= jnp.arange(n_blocks, dtype=I32)[:, None], axis=1), n_bins - 1).astype(I32)
    n_used = block_end[-1:].astype(I32)
    bin_ids = jnp.arange(n_bins, dtype=F32)[:, None]
    base_of = lambda bin_row: jnp.sum(jnp.where(bin_row[None, :] == bin_ids, slot_base[:, None], 0), axis=0)
    if by_pair:
        pair_lo, pair_hi = _pair_tables()
        block_experts = jnp.concatenate([pair_lo[block_bin], pair_hi[block_bin]])
        pos_cols = [base_of(rec_t[REC_PAIR]) + rec_t[REC_RANK_A].astype(I32)]
    else:
        block_experts = block_bin
        pos_cols = [base_of(rec_t[REC_EA]) + rec_t[REC_RANK_A].astype(I32),
                    base_of(rec_t[REC_EB]) + rec_t[REC_RANK_B].astype(I32)]

    xs = _dispatch(pos_cols, xp, n_slots)
    out_slots = _experts(block_experts, n_used, xs, w_gate, w_up, w_down, layer, n_exp)
    return _combine(pos_cols, h, out_slots, fox)


def _fox_in_body(x, gkv_ref, gmix_ref, wk_ref, wv_ref, wf_ref, bf_ref, wq_ref, wg_ref,
                 gk_ref, gq_ref, hsum_ref, hexp_ref, k_ref, v_ref, lf_ref, q_ref, sg_ref, *, tm):
    xn = x * _rms_scale(x)
    x_kv = (xn * gkv_ref[...]).astype(BF16)
    x_q = (xn * gmix_ref[...]).astype(BF16)
    def head_norm(y):
        ms = jnp.dot((y * y).astype(BF16), hsum_ref[...], preferred_element_type=F32) * (1.0 / FOX_DH)
        scale = lax.rsqrt(ms + EPS).astype(BF16)
        return y * jnp.dot(scale, hexp_ref[...], preferred_element_type=F32)

    k = jnp.dot(x_kv, wk_ref[...], preferred_element_type=F32)
    k_ref[...] = head_norm(k) * gk_ref[...]
    v_ref[...] = jnp.dot(x_kv, wv_ref[...], preferred_element_type=F32)
    fl = jnp.dot(x_kv, wf_ref[...], preferred_element_type=F32) + bf_ref[...]
    lane = lax.broadcasted_iota(I32, (tm, LANES), 1)
    lf_ref[...] = jnp.where(lane < FOX_HEADS, jax.nn.log_sigmoid(fl), 0.0)
    q = jnp.dot(x_q, wq_ref[...], preferred_element_type=F32)
    q_ref[...] = (head_norm(q) * gq_ref[...]).astype(BF16)
    gate = jnp.dot(x_q, wg_ref[...], preferred_element_type=F32)
    sg_ref[...] = jax.nn.sigmoid(gate).astype(BF16)


def _fox_in_consts(p):
    return (p["g_kv"], p["g_mix"], p["w_k"], p["w_v"], p["w_f"], p["b_f"], p["w_q"], p["w_g"],
            p["g_k"], p["g_q"], p["head_sum"], p["head_expand"])


CUMSUM_ROWS = 128


def _cumsum_kernel(x_ref, c_ref, *, seq):
    r_i = lax.broadcasted_iota(I32, (CUMSUM_ROWS, CUMSUM_ROWS), 0)
    c_i = lax.broadcasted_iota(I32, (CUMSUM_ROWS, CUMSUM_ROWS), 1)
    tri = jnp.where(c_i <= r_i, 1.0, 0.0).astype(BF16)

    def body(g, carry):
        rows = pl.ds(pl.multiple_of(g * CUMSUM_ROWS, CUMSUM_ROWS), CUMSUM_ROWS)
        x = x_ref[0, rows, :]
        x1 = x.astype(BF16)
        rem = x - x1.astype(F32)
        x2 = rem.astype(BF16)
        x3 = (rem - x2.astype(F32)).astype(BF16)
        c = (jnp.dot(tri, x1, preferred_element_type=F32) + jnp.dot(tri, x2, preferred_element_type=F32)
             + jnp.dot(tri, x3, preferred_element_type=F32)) + carry
        c_ref[0, rows, :] = c
        return c[CUMSUM_ROWS - 1:CUMSUM_ROWS, :]

    lax.fori_loop(0, seq // CUMSUM_ROWS, body, jnp.zeros((1, LANES), F32))


def _cumsum(x):
    batch, seq, _ = x.shape
    assert seq % CUMSUM_ROWS == 0
    spec = pl.BlockSpec((1, seq, LANES), lambda b: (b, 0, 0))
    return pl.pallas_call(
        functools.partial(_cumsum_kernel, seq=seq),
        grid=(batch,),
        in_specs=[spec],
        out_specs=spec,
        out_shape=jax.ShapeDtypeStruct(x.shape, F32),
        compiler_params=_cparams(("parallel",)),
        name="logf_cumsum",
    )(x)


LOG2E = 1.4426950408889634
KEY_ALIGN = 128


def _split3(x):
    x1 = x.astype(BF16).astype(F32)
    r = x - x1
    x2 = r.astype(BF16).astype(F32)
    return x1, x2, r - x2


def _fox_attn_kernel(q_ref, k_ref, v_ref, c_ref, sg_ref, o_ref, *, seq_q, tq, n_past, n_keys):
    hp = pl.program_id(1)
    lane_k = lax.broadcasted_iota(I32, (n_keys, LANES), 1)
    lane_q = lax.broadcasted_iota(I32, (seq_q, LANES), 1)
    c_pair = c_ref[0] * LOG2E
    k_pair = k_ref[...]
    v_pair = v_ref[...]
    q_pair = q_ref[...].astype(F32)

    k_aug, v_aug, q_aug, den_lane = [], [], [], []
    for hh in range(2):
        own_lo = hh * FOX_DH
        other = FOX_DH - own_lo
        c_k = jnp.sum(jnp.where(lane_k == 2 * hp + hh, c_pair, 0.0), axis=1, keepdims=True)
        c1, c2, c3 = _split3(c_k)
        d_k = lane_k - other
        extra_k = jnp.where((d_k >= 0) & (d_k < 3), 1.0,
                            jnp.where(d_k == 3, -c1, jnp.where(d_k == 4, -c2, jnp.where(d_k == 5, -c3, 0.0))))
        own_k = (lane_k >= own_lo) & (lane_k < own_lo + FOX_DH)
        k_aug.append(jnp.where(own_k, k_pair, extra_k).astype(BF16))
        v_aug.append(jnp.where(own_k, v_pair, jnp.where(d_k == 0, 1.0, 0.0)).astype(BF16))
        q1, q2, q3 = (c[n_past:n_past + seq_q] for c in (c1, c2, c3))
        d_q = lane_q - other
        extra_q = jnp.where(d_q == 0, q1, jnp.where(d_q == 1, q2, jnp.where(d_q == 2, q3,
                            jnp.where((d_q >= 3) & (d_q < 6), 1.0, 0.0))))
        own_q = (lane_q >= own_lo) & (lane_q < own_lo + FOX_DH)
        q_aug.append(jnp.where(own_q, q_pair, extra_q).astype(BF16))
        den_lane.append(other)

    nt = (((1,), (1,)), ((), ()))
    lane_o = lax.broadcasted_iota(I32, (tq, LANES), 1)
    for r0 in range(0, seq_q, tq):
        first_q = n_past + r0
        n_full = (first_q + 1) // KEY_ALIGN * KEY_ALIGN
        n_vis = min(-(-(first_q + tq) // KEY_ALIGN) * KEY_ALIGN, n_keys)
        k_pos = n_full + lax.broadcasted_iota(I32, (tq, n_vis - n_full), 1)
        q_pos = first_q + lax.broadcasted_iota(I32, (tq, n_vis - n_full), 0)
        visible = k_pos <= q_pos
        heads = []
        for hh in range(2):
            qa = q_aug[hh][r0:r0 + tq]
            s_edge = lax.dot_general(qa, k_aug[hh][n_full:n_vis], nt, preferred_element_type=F32)
            s_edge = jnp.where(visible, s_edge, -jnp.inf)
            m = jnp.max(s_edge, axis=1, keepdims=True)
            if n_full:
                s_full = lax.dot_general(qa, k_aug[hh][:n_full], nt, preferred_element_type=F32)
                m = jnp.maximum(m, jnp.max(s_full, axis=1, keepdims=True))
            acc = jnp.dot(jnp.exp2(s_edge - m).astype(BF16), v_aug[hh][n_full:n_vis],
                          preferred_element_type=F32)
            if n_full:
                acc = acc + jnp.dot(jnp.exp2(s_full - m).astype(BF16), v_aug[hh][:n_full],
                                    preferred_element_type=F32)
            denom = jnp.sum(jnp.where(lane_o == den_lane[hh], acc, 0.0), axis=1, keepdims=True)
            heads.append(acc / denom)
        o = jnp.where(lane_o < FOX_DH, heads[0], heads[1])
        o_ref[r0:r0 + tq, :] = (o * sg_ref[r0:r0 + tq, :].astype(F32)).astype(BF16)


def _fox_attention(q, sg, k_all, v_all, c_all, batch, seq_q, n_keys, n_past, tq):
    assert seq_q % tq == 0 and n_keys % KEY_ALIGN == 0
    kern = functools.partial(_fox_attn_kernel, seq_q=seq_q, tq=tq, n_past=n_past, n_keys=n_keys)
    return pl.pallas_call(
        kern,
        grid=(batch, FOX_PAIRS),
        in_specs=[pl.BlockSpec((seq_q, LANES), lambda b, hp: (b, hp)),
                  pl.BlockSpec((n_keys, LANES), lambda b, hp: (b, hp)),
                  pl.BlockSpec((n_keys, LANES), lambda b, hp: (b, hp)),
                  pl.BlockSpec((1, n_keys, LANES), lambda b, hp: (b, 0, 0)),
                  pl.BlockSpec((seq_q, LANES), lambda b, hp: (b, hp))],
        out_specs=pl.BlockSpec((seq_q, LANES), lambda b, hp: (b, hp)),
        out_shape=jax.ShapeDtypeStruct((batch * seq_q, D_MODEL), BF16),
        compiler_params=_cparams(("parallel", "parallel")),
        name="fox_attention",
    )(q, k_all, v_all, c_all, sg)


def _pad_cols(w, n):
    return jnp.pad(w, ((0, 0), (0, n - w.shape[1])))


def _prepare(norm_mix, norm_ffn, w_gla_in, w_gla_gate2, b_gla_gate, g_gla_head, w_gla_out, g_kv, w_kv,
             b_forget, g_k, w_fox_qg, g_q, w_fox_out, w_group, b_group, w_router, b_router):
    row = lambda a: a.reshape(1, -1).astype(F32)
    w_in = w_gla_in[0]
    c0, c1, c2, c3 = 2 * GLA_QK, 2 * GLA_QK + GLA_V, 2 * GLA_QK + GLA_V + GLA_GATE_RANK, w_in.shape[1]
    gla = dict(
        g=row(norm_mix[0]),
        w_qk=w_in[:, :c0].astype(BF16), w_v=w_in[:, c0:c1].astype(BF16), w_r=w_in[:, c2:c3].astype(BF16),
        w_g=_pad_cols(w_in[:, c1:c2], LANES).astype(BF16),
        w_g2=jnp.pad(w_gla_gate2[0], ((0, LANES - GLA_GATE_RANK), (0, 0))).astype(BF16),
        b_g=row(b_gla_gate[0]), g_head=row(g_gla_head[0]), w_out=w_gla_out[0].astype(BF16),
    )
    head_id = jnp.arange(D_MODEL) // FOX_DH
    fox = dict(
        g_kv=row(g_kv), g_mix=row(norm_mix[1]),
        w_k=w_kv[:, :D_MODEL].astype(BF16), w_v=w_kv[:, D_MODEL:2 * D_MODEL].astype(BF16),
        w_f=_pad_cols(w_kv[:, 2 * D_MODEL:], LANES).astype(BF16),
        b_f=_pad_cols(row(b_forget), LANES),
        w_q=w_fox_qg[0][:, :D_MODEL].astype(BF16), w_g=w_fox_qg[0][:, D_MODEL:].astype(BF16),
        g_k=row(jnp.tile(g_k, FOX_HEADS)), g_q=row(jnp.tile(g_q[0], FOX_HEADS)) * (FOX_DH ** -0.5 * LOG2E),
        head_sum=(head_id[:, None] == jnp.arange(LANES)[None, :]).astype(BF16),
        head_expand=(jnp.arange(LANES)[:, None] == head_id[None, :]).astype(BF16),
        w_out=w_fox_out[0].astype(BF16),
    )
    moe = []
    for layer in range(2):
        w_r = _pad_cols(jnp.concatenate([w_group[layer], w_router[layer]], axis=1), LANES).T
        w_r_hi = w_r.astype(BF16)
        moe.append(dict(
            g=row(norm_ffn[layer]),
            w_r_hi=w_r_hi, w_r_lo=(w_r - w_r_hi.astype(F32)).astype(BF16),
            b_r=_pad_cols(jnp.concatenate([row(b_group[layer]), row(b_router[layer])], axis=1), LANES).T,
        ))
    return gla, fox, moe


def _trunk(x, s0, past, gla, fox, moe, experts):
    batch, seq, _ = x.shape
    t = batch * seq
    h = x.reshape(t, D_MODEL)

    q, k, v, r, la = _gla_in_proj(h, gla["g"], gla["w_qk"], gla["w_v"], gla["w_r"], gla["w_g"], gla["w_g2"],
                                  gla["b_g"])
    o, s_new = _gla(q, k, v, la, s0, gla["g_head"], batch, seq)
    by_pair = _bins_are_pairs(t)
    routed = _out_proj(o, r, gla["w_out"], h, moe[0], by_pair)
    h, k_new, v_new, lf_new, qf, sg = _moe(routed, *experts, 0, fox)
    lf3 = lf_new.reshape(batch, seq, LANES)
    if past is None:
        n_past, n_keys = 0, seq
        tq = min(seq, 512)
        k_all, v_all, lf_all = k_new, v_new, lf3
    else:
        past_k, past_v, past_lf = past
        n_past = past_k.shape[1]
        tq = seq
        n_keys = -(-(n_past + seq) // KEY_ALIGN) * KEY_ALIGN
        pad = n_keys - n_past - seq
        cat = lambda a, b: jnp.concatenate(
            [a, b, jnp.zeros((batch, pad, a.shape[2]), F32)], axis=1)
        k_all = cat(past_k.reshape(batch, n_past, D_MODEL), k_new.reshape(batch, seq, D_MODEL)
                    ).reshape(batch * n_keys, D_MODEL)
        v_all = cat(past_v.reshape(batch, n_past, D_MODEL), v_new.reshape(batch, seq, D_MODEL)
                    ).reshape(batch * n_keys, D_MODEL)
        lf_all = cat(jnp.pad(past_lf, ((0, 0), (0, 0), (0, LANES - FOX_HEADS))), lf3)
    c_all = _cumsum(lf_all)
    o = _fox_attention(qf, sg, k_all, v_all, c_all, batch, seq, n_keys, n_past, tq)
    routed = _out_proj(o, None, fox["w_out"], h, moe[1], by_pair)
    h = _moe(routed, *experts, 1)

    return (h.reshape(batch, seq, D_MODEL),
            k_new.reshape(batch, seq, FOX_HEADS, FOX_DH),
            v_new.reshape(batch, seq, FOX_HEADS, FOX_DH),
            lf3[:, :, :FOX_HEADS],
            s_new[None])


def kernel(x_prompt, x_sample, cache_k, cache_v, cache_logf, state_gla, norm_mix, norm_ffn, w_gla_in, w_gla_gate2, b_gla_gate, g_gla_head, w_gla_out, g_kv, w_kv, b_forget, g_k, w_fox_qg, g_q, w_fox_out, w_group, b_group, w_router, b_router, w_exp_gate, w_exp_up, w_exp_down):
    gla, fox, moe = _prepare(norm_mix, norm_ffn, w_gla_in, w_gla_gate2, b_gla_gate, g_gla_head, w_gla_out, g_kv,
                             w_kv, b_forget, g_k, w_fox_qg, g_q, w_fox_out, w_group, b_group, w_router, b_router)
    experts = (w_exp_gate.astype(F32), w_exp_up.astype(F32), w_exp_down.astype(F32))
    s_zero = jnp.zeros((x_prompt.shape[0], GLA_HEADS, GLA_DK, GLA_DV), F32)
    y_p, k_p, v_p, lf_p, s_p = _trunk(x_prompt, s_zero, None, gla, fox, moe, experts)
    y_s, k_s, v_s, lf_s, s_s = _trunk(x_sample, state_gla[0].astype(F32),
                                      (cache_k.astype(F32), cache_v.astype(F32), cache_logf.astype(F32)),
                                      gla, fox, moe, experts)
    return (y_p, y_s, k_p, v_p, lf_p, s_p, k_s, v_s, lf_s, s_s)
```

```python
import functools

import jax
import jax.numpy as jnp
from jax import lax
from jax.experimental import pallas as pl
from jax.experimental.pallas import tpu as pltpu

F32 = jnp.float32
BF16 = jnp.bfloat16
U32 = jnp.uint32
I32 = jnp.int32

EPS = 1e-6
D_MODEL = 1024
HALF_D = D_MODEL // 2
LANES = 128
VMEM_LIMIT_BYTES = 56 * 1024 * 1024

GLA_HEADS = 4
GLA_DK = 128
GLA_DV = 256
GLA_QK = GLA_HEADS * GLA_DK
GLA_V = GLA_HEADS * GLA_DV
GLA_GATE_RANK = 16
GLA_GATE_TAU = 16.0
GLA_CHUNK = 64
GLA_GROUP = 256

FOX_HEADS = 16
FOX_DH = 64
FOX_PAIRS = FOX_HEADS // 2

N_GROUPS = 4
EXPERTS_PER_GROUP = 8
N_EXPERTS = N_GROUPS * EXPERTS_PER_GROUP
D_EXPERT = 512
MOE_BLOCK = 256
ROW_DMA_UNROLL = 16


def _cparams(sem):
    return pltpu.CompilerParams(dimension_semantics=sem, vmem_limit_bytes=VMEM_LIMIT_BYTES)


def _row_tile(t, pref):
    tm = min(t, pref)
    assert t % tm == 0
    return tm


def _resident(a):
    return pl.BlockSpec(a.shape, lambda *_: (0,) * a.ndim, pipeline_mode=pl.Buffered(1))


def _rms_scale(x):
    return lax.rsqrt(jnp.mean(x * x, axis=-1, keepdims=True) + EPS)


def _pack_halves(y):
    lo = lax.bitcast_convert_type(y[:, :HALF_D].astype(BF16).astype(F32), U32)
    hi = lax.bitcast_convert_type(y[:, HALF_D:].astype(BF16).astype(F32), U32)
    return (lo >> 16) | (hi & jnp.uint32(0xFFFF0000))


def _unpack_halves(p):
    lo = lax.bitcast_convert_type(p << 16, F32)
    hi = lax.bitcast_convert_type(p & jnp.uint32(0xFFFF0000), F32)
    return lo, hi


TILE_ROWS = 8
ROW_PIECES = D_MODEL // LANES
assert ROW_PIECES == TILE_ROWS


def _load_tiled_rows(ref, n, pieces):
    return jnp.concatenate([ref[pl.ds(j, n, stride=TILE_ROWS), :] for j in pieces], axis=1)


def _store_tiled_rows(ref, val):
    n = val.shape[0]
    for j in range(ROW_PIECES):
        ref[pl.ds(j, n, stride=TILE_ROWS), :] = val[:, j * LANES:(j + 1) * LANES]


def _gla_in_kernel(h_ref, g_ref, wqk_ref, wv_ref, wr_ref, wg_ref, wg2_ref, bg_ref,
                   q_ref, k_ref, v_ref, r_ref, la_ref):
    x = h_ref[...]
    hn = (x * _rms_scale(x) * g_ref[...]).astype(BF16)
    qk = jnp.dot(hn, wqk_ref[...], preferred_element_type=F32)
    q_ref[...] = (qk[:, :GLA_QK] * (GLA_DK ** -0.5)).astype(BF16)
    k_ref[...] = qk[:, GLA_QK:].astype(BF16)
    v_ref[...] = jnp.dot(hn, wv_ref[...], preferred_element_type=F32).astype(BF16)
    r_ref[...] = jnp.dot(hn, wr_ref[...], preferred_element_type=F32).astype(BF16)
    g_lr = jnp.dot(hn, wg_ref[...], preferred_element_type=F32)
    z = jnp.dot(g_lr.astype(BF16), wg2_ref[...], preferred_element_type=F32) + bg_ref[...]
    la_ref[...] = jax.nn.log_sigmoid(z) / GLA_GATE_TAU


def _gla_in_proj(h, g, wqk, wv, wr, wg, wg2, bg):
    t = h.shape[0]
    tm = _row_tile(t, 512)
    row = lambda n: pl.BlockSpec((tm, n), lambda i: (i, 0))
    full = _resident
    return pl.pallas_call(
        _gla_in_kernel,
        grid=(t // tm,),
        in_specs=[row(D_MODEL), full(g), full(wqk), full(wv), full(wr), full(wg), full(wg2), full(bg)],
        out_specs=[row(GLA_QK), row(GLA_QK), row(GLA_V), row(GLA_V), row(GLA_QK)],
        out_shape=[jax.ShapeDtypeStruct((t, GLA_QK), BF16), jax.ShapeDtypeStruct((t, GLA_QK), BF16),
                   jax.ShapeDtypeStruct((t, GLA_V), BF16), jax.ShapeDtypeStruct((t, GLA_V), BF16),
                   jax.ShapeDtypeStruct((t, GLA_QK), F32)],
        compiler_params=_cparams(("parallel",)),
        name="gla_in_proj",
    )(h, g, wqk, wv, wr, wg, wg2, bg)


def _gla_kernel(q_ref, k_ref, v_ref, la_ref, s0_ref, gh_ref, o_ref, s_ref,
                qb_s, oi_s, u_s, dl_s, *, seq, chunk, group):
    n_groups = seq // group
    n_chunks = seq // chunk
    per_group = group // chunk

    shift = chunk.bit_length() - 1
    assert chunk == 1 << shift and group & (group - 1) == 0
    row = lax.broadcasted_iota(I32, (group, group), 0)
    col = lax.broadcasted_iota(I32, (group, group), 1)
    tril = ((row >> shift) == (col >> shift)) & (col <= row)
    tril_bf = jnp.where(tril, 1.0, 0.0).astype(BF16)

    def chunk_rows(b, at):
        return jnp.concatenate(
            [jnp.broadcast_to(b[j * chunk + at:j * chunk + at + 1], (chunk, GLA_DK)) for j in range(per_group)],
            axis=0)

    def group_body(g, carry):
        r0 = pl.multiple_of(g * group, group)
        rows = pl.ds(r0, group)
        la = la_ref[rows, :]
        la_hi = la.astype(BF16)
        la_lo = (la - la_hi.astype(F32)).astype(BF16)
        b = (jnp.dot(tril_bf, la_hi, preferred_element_type=F32)
             + jnp.dot(tril_bf, la_lo, preferred_element_type=F32))
        b_ref = chunk_rows(b, chunk // 2)
        b_last = chunk_rows(b, chunk - 1)
        qf = q_ref[rows, :].astype(F32)
        kf = k_ref[rows, :].astype(F32)
        vg = v_ref[rows, :]
        qe = (qf * jnp.exp(b - b_ref)).astype(BF16)
        ke = (kf * jnp.exp(b_ref - b)).astype(BF16)
        att = lax.dot_general(qe, ke, (((1,), (1,)), ((), ())), preferred_element_type=F32)
        att = jnp.where(tril, att, 0.0).astype(BF16)
        oi_s[rows, :] = jnp.dot(att, vg, preferred_element_type=F32)
        qb_s[rows, :] = (qf * jnp.exp(b)).astype(BF16)
        kl = (kf * jnp.exp(b_last - b)).astype(BF16)
        decay = jnp.exp(b_last)
        for j in range(per_group):
            c = g * per_group + j
            sl = slice(j * chunk, (j + 1) * chunk)
            u_s[c] = lax.dot_general(vg[sl], kl[sl], (((0,), (0,)), ((), ())), preferred_element_type=F32)
            dl_s[c] = decay[j * chunk:j * chunk + 1]
        return carry

    lax.fori_loop(0, n_groups, group_body, 0, unroll=min(2, n_groups))

    gh = gh_ref[...]

    def chunk_body(c, st):
        r0 = pl.multiple_of(c * chunk, chunk)
        rows = pl.ds(r0, chunk)
        o = oi_s[rows, :] + lax.dot_general(qb_s[rows, :], st.astype(BF16), (((1,), (1,)), ((), ())),
                                            preferred_element_type=F32)
        o_ref[rows, :] = (o * _rms_scale(o) * gh).astype(BF16)
        return st * dl_s[c] + u_s[c]

    st = lax.fori_loop(0, n_chunks, chunk_body, s0_ref[0, 0].T, unroll=min(8, n_chunks))
    s_ref[0, 0] = st.T


def _gla(q, k, v, la, s0, g_head, batch, seq):
    chunk = min(seq, GLA_CHUNK)
    group = min(seq, GLA_GROUP)
    assert seq % group == 0 and group % chunk == 0
    n_chunks = seq // chunk
    kern = functools.partial(_gla_kernel, seq=seq, chunk=chunk, group=group)
    return pl.pallas_call(
        kern,
        grid=(batch, GLA_HEADS),
        in_specs=[pl.BlockSpec((seq, GLA_DK), lambda b, h: (b, h)),
                  pl.BlockSpec((seq, GLA_DK), lambda b, h: (b, h)),
                  pl.BlockSpec((seq, GLA_DV), lambda b, h: (b, h)),
                  pl.BlockSpec((seq, GLA_DK), lambda b, h: (b, h)),
                  pl.BlockSpec((1, 1, GLA_DK, GLA_DV), lambda b, h: (b, h, 0, 0)),
                  pl.BlockSpec((1, GLA_DV), lambda b, h: (0, 0))],
        out_specs=[pl.BlockSpec((seq, GLA_DV), lambda b, h: (b, h)),
                   pl.BlockSpec((1, 1, GLA_DK, GLA_DV), lambda b, h: (b, h, 0, 0))],
        out_shape=[jax.ShapeDtypeStruct((batch * seq, GLA_V), BF16),
                   jax.ShapeDtypeStruct((batch, GLA_HEADS, GLA_DK, GLA_DV), F32)],
        scratch_shapes=[pltpu.VMEM((seq, GLA_DK), BF16),
                        pltpu.VMEM((seq, GLA_DV), F32),
                        pltpu.VMEM((n_chunks, GLA_DV, GLA_DK), F32),
                        pltpu.VMEM((n_chunks, 1, GLA_DK), F32)],
        compiler_params=_cparams(("parallel", "parallel")),
        name="gla_scan",
    )(q, k, v, la, s0, g_head)


def _out_proj_kernel(*refs, gated, tm, by_pair):
    if gated:
        a_ref, r_ref = refs[:2]
        rest = refs[2:]
        r = r_ref[...].astype(F32)
        a = (a_ref[...].astype(F32) * (r * jax.nn.sigmoid(r))).astype(BF16)
    else:
        rest = refs[1:]
        a = refs[0][...]
    w_ref, h_ref = rest[:2]
    router_refs = rest[2:6]
    o_ref = rest[6]
    y = h_ref[...] + jnp.dot(a, w_ref[...], preferred_element_type=F32)
    o_ref[...] = y
    _router_body(y, *router_refs, *rest[7:], tm=tm, by_pair=by_pair)


def _out_proj(a, r, w, h, moe_p, by_pair):
    t = h.shape[0]
    tm = _row_tile(t, 512)
    row = pl.BlockSpec((tm, D_MODEL), lambda i: (i, 0))
    gated = r is not None
    router_consts = (moe_p["g"], moe_p["w_r_hi"], moe_p["w_r_lo"], moe_p["b_r"])
    args = ((a, r) if gated else (a,)) + (w, h) + router_consts
    specs = ([row, row] if gated else [row]) + [_resident(w), row] + [_resident(c) for c in router_consts]
    return pl.pallas_call(
        functools.partial(_out_proj_kernel, gated=gated, tm=tm, by_pair=by_pair),
        grid=(t // tm,),
        in_specs=specs,
        out_specs=[row,
                   pl.BlockSpec((tm * TILE_ROWS, LANES), lambda i: (i, 0)),
                   pl.BlockSpec((REC_WIDTH, tm), lambda i: (0, i)),
                   pl.BlockSpec((LANES, 1), lambda i: (0, 0))],
        out_shape=[jax.ShapeDtypeStruct((t, D_MODEL), F32),
                   jax.ShapeDtypeStruct((t * TILE_ROWS, LANES), U32),
                   jax.ShapeDtypeStruct((REC_WIDTH, t), F32),
                   jax.ShapeDtypeStruct((LANES, 1), F32)],
        scratch_shapes=[pltpu.VMEM((LANES, 1), F32)],
        compiler_params=_cparams(("arbitrary",)),
        name="out_proj_gated_router" if gated else "out_proj_router",
    )(*args)


REC_EA, REC_EB, REC_PAIR, REC_RANK_A, REC_RANK_B = range(5)
REC_WIDTH = 8
META_WA, META_WB, META_EA, META_EB = range(4)
META_PIECE = HALF_D // LANES
IN_PIECES = range(META_PIECE + 1)
LOGIT_ROWS = -(-(N_GROUPS + N_EXPERTS) // TILE_ROWS) * TILE_ROWS
PAIRS_PER_GROUP = EXPERTS_PER_GROUP * (EXPERTS_PER_GROUP - 1) // 2
N_PAIRS = N_GROUPS * PAIRS_PER_GROUP


def _router_body(x, g_ref, wt_hi_ref, wt_lo_ref, bias_ref, xp_ref, rec_t_ref, cnt_ref, carry_s, *, tm, by_pair):
    @pl.when(pl.program_id(0) == 0)
    def _():
        carry_s[...] = jnp.zeros_like(carry_s)

    hn = x * _rms_scale(x) * g_ref[...]

    hi = hn.astype(BF16)
    lo = (hn - hi.astype(F32)).astype(BF16)
    nt = (((1,), (1,)), ((), ()))
    wt_hi = wt_hi_ref[...]
    lg_all = (lax.dot_general(wt_hi, hi, nt, preferred_element_type=F32)
              + lax.dot_general(wt_hi, lo, nt, preferred_element_type=F32)
              + lax.dot_general(wt_lo_ref[...], hi, nt, preferred_element_type=F32) + bias_ref[...])
    lg = lg_all[:LOGIT_ROWS]

    sub = lax.broadcasted_iota(I32, (LOGIT_ROWS, tm), 0).astype(F32)
    neg = jnp.float32(-jnp.inf)

    def masked_softmax(mask):
        m = jnp.max(jnp.where(mask, lg, neg), axis=0, keepdims=True)
        e = jnp.where(mask, jnp.exp(lg - m), 0.0)
        return e / jnp.sum(e, axis=0, keepdims=True)

    def top1(p, mask):
        v = jnp.max(jnp.where(mask, p, -1.0), axis=0, keepdims=True)
        idx = jnp.min(jnp.where(mask & (p == v), sub, float(LANES)), axis=0, keepdims=True)
        return v, idx

    gmask = sub < N_GROUPS
    g_top, g_idx = top1(masked_softmax(gmask), gmask)
    e_lo = N_GROUPS + EXPERTS_PER_GROUP * g_idx
    emask = (sub >= e_lo) & (sub < e_lo + EXPERTS_PER_GROUP)
    ep = masked_softmax(emask)
    p1, i1 = top1(ep, emask)
    mask2 = emask & (sub != i1)
    p2, i2 = top1(ep, mask2)
    denom = p1 + p2
    w0 = g_top * p1 / denom
    w1 = g_top * p2 / denom

    a0 = i1 - e_lo
    a1 = i2 - e_lo
    first_low = a0 < a1
    lo_l = jnp.where(first_low, a0, a1)
    hi_l = jnp.where(first_low, a1, a0)
    w_a = jnp.where(first_low, w0, w1)
    w_b = jnp.where(first_low, w1, w0)
    e_a = EXPERTS_PER_GROUP * g_idx + lo_l
    e_b = EXPERTS_PER_GROUP * g_idx + hi_l
    pair = (PAIRS_PER_GROUP * g_idx + lo_l * (2 * EXPERTS_PER_GROUP - 1 - lo_l) * 0.5 + (hi_l - lo_l - 1.0))

    def stack_rows(vals):
        row8 = lax.broadcasted_iota(I32, (TILE_ROWS, tm), 0)
        out = jnp.zeros((TILE_ROWS, tm), F32)
        for k, val in enumerate(vals):
            out = jnp.where(row8 == k, val, out)
        return out

    meta_t = jnp.concatenate([stack_rows([w_a, w_b, e_a, e_b]), jnp.zeros((LANES - TILE_ROWS, tm), F32)], axis=0)
    _store_tiled_rows(xp_ref, jnp.concatenate(
        [_pack_halves(hn), lax.bitcast_convert_type(meta_t.T, U32),
         jnp.zeros((tm, D_MODEL - HALF_D - LANES), U32)], axis=1))

    bins = lax.broadcasted_iota(I32, (LANES, tm), 0).astype(F32)
    if by_pair:
        hit_a = bins == pair
        onehot = jnp.where(hit_a, 1.0, 0.0)
    else:
        hit_a = bins == e_a
        hit_b = bins == e_b
        onehot = jnp.where(hit_a | hit_b, 1.0, 0.0)
    r_i = lax.broadcasted_iota(I32, (tm, tm), 0)
    c_i = lax.broadcasted_iota(I32, (tm, tm), 1)
    earlier = jnp.where(r_i < c_i, 1.0, 0.0).astype(BF16)
    prior = jnp.dot(onehot.astype(BF16), earlier, preferred_element_type=F32) + carry_s[...]
    rank_a = jnp.sum(jnp.where(hit_a, prior, 0.0), axis=0, keepdims=True)
    rank_b = jnp.zeros_like(rank_a) if by_pair else jnp.sum(jnp.where(hit_b, prior, 0.0), axis=0, keepdims=True)
    carry_s[...] = carry_s[...] + jnp.sum(onehot, axis=1, keepdims=True)
    cnt_ref[...] = carry_s[...]

    rec_t_ref[...] = stack_rows([e_a, e_b, pair, rank_a, rank_b])


def _row_copy(src, src_row, dst, dst_row, sem):
    tile = lambda ref, row: ref.at[pl.ds(pl.multiple_of(row * TILE_ROWS, TILE_ROWS), TILE_ROWS)]
    return pltpu.make_async_copy(tile(src, src_row), tile(dst, dst_row), sem)


def _for_each_row(tm, n_pos, fn):
    rows_per_trip = ROW_DMA_UNROLL // n_pos

    def trip(i, carry):
        for j in range(rows_per_trip):
            for k in range(n_pos):
                fn(i * rows_per_trip + j, k, j * n_pos + k)
        return carry

    lax.fori_loop(0, tm // rows_per_trip, trip, 0)


def _dispatch_kernel(pos_ref, xp_ref, zeros_hbm, xs_hbm, sem, *, tm, n_pos):
    del zeros_hbm
    _for_each_row(tm, n_pos, lambda r, k, n: _row_copy(
        xp_ref, r, xs_hbm, pos_ref[0, 0, k * tm + r], sem.at[0]).start(priority=n % 2))
    _for_each_row(tm, n_pos, lambda r, k, n: _row_copy(xp_ref, 0, xs_hbm, 0, sem.at[0]).wait())


def _dispatch(pos_cols, xp, n_slots):
    t = xp.shape[0] // TILE_ROWS
    tm = _row_tile(t, 256)
    n_pos = len(pos_cols)
    pos = jnp.concatenate([p.reshape(t // tm, 1, tm) for p in pos_cols], axis=2)
    return pl.pallas_call(
        functools.partial(_dispatch_kernel, tm=tm, n_pos=n_pos),
        grid=(t // tm,),
        in_specs=[pl.BlockSpec((1, 1, n_pos * tm), lambda i: (i, 0, 0), memory_space=pltpu.SMEM),
                  pl.BlockSpec((tm * TILE_ROWS, LANES), lambda i: (i, 0)),
                  pl.BlockSpec(memory_space=pl.ANY)],
        out_specs=pl.BlockSpec(memory_space=pl.ANY),
        out_shape=jax.ShapeDtypeStruct((n_slots * TILE_ROWS, LANES), U32),
        input_output_aliases={2: 0},
        scratch_shapes=[pltpu.SemaphoreType.DMA((1,))],
        compiler_params=_cparams(("arbitrary",)),
        name="moe_dispatch",
    )(pos, xp, jnp.zeros((n_slots * TILE_ROWS, LANES), U32))


def _expert_kernel(*refs, n_exp, n_blocks):
    be_ref, nb_ref, xs_ref = refs[:3]
    w_refs = refs[3:3 + 3 * n_exp]
    out_ref = refs[3 + 3 * n_exp]
    scratch = refs[4 + 3 * n_exp:]
    b = pl.program_id(0)

    @pl.when(b < nb_ref[0])
    def _():
        prev = jnp.maximum(b - 1, 0)
        for j in range(n_exp):
            wg_ref, wu_ref, wd_ref = w_refs[3 * j:3 * j + 3]
            wgu_s, wd_s = scratch[2 * j:2 * j + 2]

            @pl.when((b == 0) | (be_ref[j * n_blocks + b] != be_ref[j * n_blocks + prev]))
            def _():
                wgu_s[:, :D_EXPERT] = wg_ref[0, 0].astype(BF16)
                wgu_s[:, D_EXPERT:] = wu_ref[0, 0].astype(BF16)
                wd_s[...] = wd_ref[0, 0].astype(BF16)

        rows = _load_tiled_rows(xs_ref, MOE_BLOCK, IN_PIECES)
        lo, hi = _unpack_halves(rows[:, :HALF_D])
        x_lo = lo.astype(BF16)
        x_hi = hi.astype(BF16)
        meta = lax.bitcast_convert_type(rows[:, HALF_D:], F32)
        y = None
        for j in range(n_exp):
            wgu_s, wd_s = scratch[2 * j:2 * j + 2]
            gu = (jnp.dot(x_lo, wgu_s[:HALF_D, :], preferred_element_type=F32)
                  + jnp.dot(x_hi, wgu_s[HALF_D:, :], preferred_element_type=F32))
            gate = gu[:, :D_EXPERT]
            if n_exp == 2:
                w = meta[:, META_WA + j:META_WA + j + 1]
            else:
                this = be_ref[b].astype(F32)
                w = jnp.where(meta[:, META_EA:META_EA + 1] == this,
                              meta[:, META_WA:META_WA + 1], meta[:, META_WB:META_WB + 1])
            hid = (w * (gate * jax.nn.sigmoid(gate) * gu[:, D_EXPERT:])).astype(BF16)
            y_j = jnp.dot(hid, wd_s[...], preferred_element_type=F32)
            y = y_j if y is None else y + y_j
        _store_tiled_rows(out_ref, y)

    @pl.when(b >= nb_ref[0])
    def _():
        out_ref[...] = jnp.zeros_like(out_ref)


def _experts(block_experts, n_blocks_used, xs, w_gate, w_up, w_down, layer, n_exp):
    n_slots = xs.shape[0] // TILE_ROWS
    n_blocks = n_slots // MOE_BLOCK
    block_rows = MOE_BLOCK * TILE_ROWS

    def used(b, nb):
        return jnp.minimum(b, nb[0] - 1)

    def w_spec(shape, j):
        return pl.BlockSpec((1, 1) + shape, lambda b, be, nb: (layer, be[j * n_blocks + used(b, nb)], 0, 0))

    w_specs, w_args, scratch = [], [], []
    for j in range(n_exp):
        w_specs += [w_spec((D_MODEL, D_EXPERT), j), w_spec((D_MODEL, D_EXPERT), j), w_spec((D_EXPERT, D_MODEL), j)]
        w_args += [w_gate, w_up, w_down]
        scratch += [pltpu.VMEM((D_MODEL, 2 * D_EXPERT), BF16), pltpu.VMEM((D_EXPERT, D_MODEL), BF16)]
    grid_spec = pltpu.PrefetchScalarGridSpec(
        num_scalar_prefetch=2,
        grid=(n_blocks,),
        in_specs=[pl.BlockSpec((block_rows, LANES), lambda b, be, nb: (used(b, nb), 0))] + w_specs,
        out_specs=pl.BlockSpec((block_rows, LANES), lambda b, be, nb: (b, 0)),
        scratch_shapes=scratch,
    )
    return pl.pallas_call(
        functools.partial(_expert_kernel, n_exp=n_exp, n_blocks=n_blocks),
        grid_spec=grid_spec,
        out_shape=jax.ShapeDtypeStruct((n_slots * TILE_ROWS, LANES), F32),
        compiler_params=_cparams(("arbitrary",)),
        name="moe_experts",
    )(block_experts, n_blocks_used, xs, *w_args)


def _combine_kernel(*refs, tm, n_pos, n_fox_consts):
    pos_ref, pos_next_ref, h_ref, out_hbm = refs[:4]
    fox_consts = refs[4:4 + n_fox_consts]
    n_out = 6 if n_fox_consts else 1
    outs = refs[4 + n_fox_consts:4 + n_fox_consts + n_out]
    gathered = refs[4 + n_fox_consts + n_out:-1]
    sem = refs[-1]
    i = pl.program_id(0)
    slot = i % 2

    def issue(p_ref, s):
        _for_each_row(tm, n_pos, lambda r, k, n: _row_copy(
            out_hbm, p_ref[0, 0, k * tm + r], gathered[k].at[s], r, sem.at[s]).start(priority=n % 2))

    @pl.when(i == 0)
    def _():
        issue(pos_ref, 0)

    @pl.when(i + 1 < pl.num_programs(0))
    def _():
        issue(pos_next_ref, 1 - slot)

    _for_each_row(tm, n_pos, lambda r, k, n: _row_copy(
        out_hbm, 0, gathered[k].at[slot], 0, sem.at[slot]).wait())

    y = h_ref[...]
    for g_s in gathered:
        y = y + _load_tiled_rows(g_s.at[slot], tm, range(ROW_PIECES))
    outs[0][...] = y
    if n_fox_consts:
        _fox_in_body(y, *fox_consts, *outs[1:], tm=tm)


def _combine(pos_cols, h, out_slots, fox=None):
    t = h.shape[0]
    tm = _row_tile(t, 512)
    nt = t // tm
    n_pos = len(pos_cols)
    pos = jnp.concatenate([p.reshape(nt, 1, tm) for p in pos_cols], axis=2)
    row = lambda n: pl.BlockSpec((tm, n), lambda i: (i, 0))
    pos_spec = lambda step: pl.BlockSpec((1, 1, n_pos * tm), lambda i: (jnp.minimum(i + step, nt - 1), 0, 0),
                                         memory_space=pltpu.SMEM)
    consts = () if fox is None else _fox_in_consts(fox)
    out_specs = [row(D_MODEL)]
    out_shape = [jax.ShapeDtypeStruct((t, D_MODEL), F32)]
    if fox is not None:
        out_specs += [row(D_MODEL), row(D_MODEL), row(LANES), row(D_MODEL), row(D_MODEL)]
        out_shape += [jax.ShapeDtypeStruct((t, D_MODEL), F32), jax.ShapeDtypeStruct((t, D_MODEL), F32),
                      jax.ShapeDtypeStruct((t, LANES), F32), jax.ShapeDtypeStruct((t, D_MODEL), BF16),
                      jax.ShapeDtypeStruct((t, D_MODEL), BF16)]
    res = pl.pallas_call(
        functools.partial(_combine_kernel, tm=tm, n_pos=n_pos, n_fox_consts=len(consts)),
        grid=(nt,),
        in_specs=[pos_spec(0), pos_spec(1), row(D_MODEL), pl.BlockSpec(memory_space=pl.ANY)]
                 + [_resident(a) for a in consts],
        out_specs=out_specs,
        out_shape=out_shape,
        scratch_shapes=[pltpu.VMEM((2, tm * TILE_ROWS, LANES), F32)] * n_pos + [pltpu.SemaphoreType.DMA((2,))],
        compiler_params=_cparams(("arbitrary",)),
        name="moe_combine" if fox is None else "moe_combine_fox_in",
    )(pos, pos, h, out_slots, *consts)
    return res[0] if fox is None else res


def _pair_tables():
    pairs = [(a, b) for a in range(EXPERTS_PER_GROUP) for b in range(a + 1, EXPERTS_PER_GROUP)]
    lo = [EXPERTS_PER_GROUP * g + a for g in range(N_GROUPS) for a, _ in pairs]
    hi = [EXPERTS_PER_GROUP * g + b for g in range(N_GROUPS) for _, b in pairs]
    return jnp.array(lo, I32), jnp.array(hi, I32)


def _bins_are_pairs(t):
    return t // MOE_BLOCK >= 2 * N_PAIRS


def _moe(routed, w_gate, w_up, w_down, layer, fox=None):
    h, xp, rec_t, cnt = routed
    t = h.shape[0]
    by_pair = _bins_are_pairs(t)
    n_bins, n_pos, n_exp = (N_PAIRS, 1, 2) if by_pair else (N_EXPERTS, 2, 1)

    counts = cnt[:n_bins, 0].astype(I32)
    blocks_per_bin = (counts + MOE_BLOCK - 1) // MOE_BLOCK
    block_end = jnp.cumsum(blocks_per_bin)
    slot_base = (block_end - blocks_per_bin) * MOE_BLOCK
    n_blocks = n_pos * t // MOE_BLOCK + n_bins
    n_slots = n_blocks * MOE_BLOCK
    block_bin = jnp.minimum(
        jnp.sum(block_end[None, :] <= jnp.arange(n_blocks, dtype=I32)[:, None], axis=1), n_bins - 1).astype(I32)
    n_used = block_end[-1:].astype(I32)
    bin_ids = jnp.arange(n_bins, dtype=F32)[:, None]
    base_of = lambda bin_row: jnp.sum(jnp.where(bin_row[None, :] == bin_ids, slot_base[:, None], 0), axis=0)
    if by_pair:
        pair_lo, pair_hi = _pair_tables()
        block_experts = jnp.concatenate([pair_lo[block_bin], pair_hi[block_bin]])
        pos_cols = [base_of(rec_t[REC_PAIR]) + rec_t[REC_RANK_A].astype(I32)]
    else:
        block_experts = block_bin
        pos_cols = [base_of(rec_t[REC_EA]) + rec_t[REC_RANK_A].astype(I32),
                    base_of(rec_t[REC_EB]) + rec_t[REC_RANK_B].astype(I32)]

    xs = _dispatch(pos_cols, xp, n_slots)
    out_slots = _experts(block_experts, n_used, xs, w_gate, w_up, w_down, layer, n_exp)
    return _combine(pos_cols, h, out_slots, fox)


def _fox_in_body(x, gkv_ref, gmix_ref, wk_ref, wv_ref, wf_ref, bf_ref, wq_ref, wg_ref,
                 gk_ref, gq_ref, hsum_ref, hexp_ref, k_ref, v_ref, lf_ref, q_ref, sg_ref, *, tm):
    xn = x * _rms_scale(x)
    x_kv = (xn * gkv_ref[...]).astype(BF16)
    x_q = (xn * gmix_ref[...]).astype(BF16)
    def head_norm(y):
        ms = jnp.dot((y * y).astype(BF16), hsum_ref[...], preferred_element_type=F32) * (1.0 / FOX_DH)
        scale = lax.rsqrt(ms + EPS).astype(BF16)
        return y * jnp.dot(scale, hexp_ref[...], preferred_element_type=F32)

    k = jnp.dot(x_kv, wk_ref[...], preferred_element_type=F32)
    k_ref[...] = head_norm(k) * gk_ref[...]
    v_ref[...] = jnp.dot(x_kv, wv_ref[...], preferred_element_type=F32)
    fl = jnp.dot(x_kv, wf_ref[...], preferred_element_type=F32) + bf_ref[...]
    lane = lax.broadcasted_iota(I32, (tm, LANES), 1)
    lf_ref[...] = jnp.where(lane < FOX_HEADS, jax.nn.log_sigmoid(fl), 0.0)
    q = jnp.dot(x_q, wq_ref[...], preferred_element_type=F32)
    q_ref[...] = (head_norm(q) * gq_ref[...]).astype(BF16)
    gate = jnp.dot(x_q, wg_ref[...], preferred_element_type=F32)
    sg_ref[...] = jax.nn.sigmoid(gate).astype(BF16)


def _fox_in_consts(p):
    return (p["g_kv"], p["g_mix"], p["w_k"], p["w_v"], p["w_f"], p["b_f"], p["w_q"], p["w_g"],
            p["g_k"], p["g_q"], p["head_sum"], p["head_expand"])


CUMSUM_ROWS = 128


def _cumsum_kernel(x_ref, c_ref, *, seq):
    r_i = lax.broadcasted_iota(I32, (CUMSUM_ROWS, CUMSUM_ROWS), 0)
    c_i = lax.broadcasted_iota(I32, (CUMSUM_ROWS, CUMSUM_ROWS), 1)
    tri = jnp.where(c_i <= r_i, 1.0, 0.0).astype(BF16)

    def body(g, carry):
        rows = pl.ds(pl.multiple_of(g * CUMSUM_ROWS, CUMSUM_ROWS), CUMSUM_ROWS)
        x = x_ref[0, rows, :]
        x1 = x.astype(BF16)
        rem = x - x1.astype(F32)
        x2 = rem.astype(BF16)
        x3 = (rem - x2.astype(F32)).astype(BF16)
        c = (jnp.dot(tri, x1, preferred_element_type=F32) + jnp.dot(tri, x2, preferred_element_type=F32)
             + jnp.dot(tri, x3, preferred_element_type=F32)) + carry
        c_ref[0, rows, :] = c
        return c[CUMSUM_ROWS - 1:CUMSUM_ROWS, :]

    lax.fori_loop(0, seq // CUMSUM_ROWS, body, jnp.zeros((1, LANES), F32))


def _cumsum(x):
    batch, seq, _ = x.shape
    assert seq % CUMSUM_ROWS == 0
    spec = pl.BlockSpec((1, seq, LANES), lambda b: (b, 0, 0))
    return pl.pallas_call(
        functools.partial(_cumsum_kernel, seq=seq),
        grid=(batch,),
        in_specs=[spec],
        out_specs=spec,
        out_shape=jax.ShapeDtypeStruct(x.shape, F32),
        compiler_params=_cparams(("parallel",)),
        name="logf_cumsum",
    )(x)


LOG2E = 1.4426950408889634
KEY_ALIGN = 128


def _split3(x):
    x1 = x.astype(BF16).astype(F32)
    r = x - x1
    x2 = r.astype(BF16).astype(F32)
    return x1, x2, r - x2


def _fox_attn_kernel(q_ref, k_ref, v_ref, c_ref, sg_ref, o_ref, *, seq_q, tq, n_past, n_keys):
    hp = pl.program_id(1)
    lane_k = lax.broadcasted_iota(I32, (n_keys, LANES), 1)
    lane_q = lax.broadcasted_iota(I32, (seq_q, LANES), 1)
    c_pair = c_ref[0] * LOG2E
    k_pair = k_ref[...]
    v_pair = v_ref[...]
    q_pair = q_ref[...].astype(F32)

    k_aug, v_aug, q_aug, den_lane = [], [], [], []
    for hh in range(2):
        own_lo = hh * FOX_DH
        other = FOX_DH - own_lo
        c_k = jnp.sum(jnp.where(lane_k == 2 * hp + hh, c_pair, 0.0), axis=1, keepdims=True)
        c1, c2, c3 = _split3(c_k)
        d_k = lane_k - other
        extra_k = jnp.where((d_k >= 0) & (d_k < 3), 1.0,
                            jnp.where(d_k == 3, -c1, jnp.where(d_k == 4, -c2, jnp.where(d_k == 5, -c3, 0.0))))
        own_k = (lane_k >= own_lo) & (lane_k < own_lo + FOX_DH)
        k_aug.append(jnp.where(own_k, k_pair, extra_k).astype(BF16))
        v_aug.append(jnp.where(own_k, v_pair, jnp.where(d_k == 0, 1.0, 0.0)).astype(BF16))
        q1, q2, q3 = (c[n_past:n_past + seq_q] for c in (c1, c2, c3))
        d_q = lane_q - other
        extra_q = jnp.where(d_q == 0, q1, jnp.where(d_q == 1, q2, jnp.where(d_q == 2, q3,
                            jnp.where((d_q >= 3) & (d_q < 6), 1.0, 0.0))))
        own_q = (lane_q >= own_lo) & (lane_q < own_lo + FOX_DH)
        q_aug.append(jnp.where(own_q, q_pair, extra_q).astype(BF16))
        den_lane.append(other)

    nt = (((1,), (1,)), ((), ()))
    lane_o = lax.broadcasted_iota(I32, (tq, LANES), 1)
    for r0 in range(0, seq_q, tq):
        first_q = n_past + r0
        n_full = (first_q + 1) // KEY_ALIGN * KEY_ALIGN
        n_vis = min(-(-(first_q + tq) // KEY_ALIGN) * KEY_ALIGN, n_keys)
        k_pos = n_full + lax.broadcasted_iota(I32, (tq, n_vis - n_full), 1)
        q_pos = first_q + lax.broadcasted_iota(I32, (tq, n_vis - n_full), 0)
        visible = k_pos <= q_pos
        heads = []
        for hh in range(2):
            qa = q_aug[hh][r0:r0 + tq]
            s_edge = lax.dot_general(qa, k_aug[hh][n_full:n_vis], nt, preferred_element_type=F32)
            s_edge = jnp.where(visible, s_edge, -jnp.inf)
            m = jnp.max(s_edge, axis=1, keepdims=True)
            if n_full:
                s_full = lax.dot_general(qa, k_aug[hh][:n_full], nt, preferred_element_type=F32)
                m = jnp.maximum(m, jnp.max(s_full, axis=1, keepdims=True))
            acc = jnp.dot(jnp.exp2(s_edge - m).astype(BF16), v_aug[hh][n_full:n_vis],
                          preferred_element_type=F32)
            if n_full:
                acc = acc + jnp.dot(jnp.exp2(s_full - m).astype(BF16), v_aug[hh][:n_full],
                                    preferred_element_type=F32)
            denom = jnp.sum(jnp.where(lane_o == den_lane[hh], acc, 0.0), axis=1, keepdims=True)
            heads.append(acc / denom)
        o = jnp.where(lane_o < FOX_DH, heads[0], heads[1])
        o_ref[r0:r0 + tq, :] = (o * sg_ref[r0:r0 + tq, :].astype(F32)).astype(BF16)


def _fox_attention(q, sg, k_all, v_all, c_all, batch, seq_q, n_keys, n_past, tq):
    assert seq_q % tq == 0 and n_keys % KEY_ALIGN == 0
    kern = functools.partial(_fox_attn_kernel, seq_q=seq_q, tq=tq, n_past=n_past, n_keys=n_keys)
    return pl.pallas_call(
        kern,
        grid=(batch, FOX_PAIRS),
        in_specs=[pl.BlockSpec((seq_q, LANES), lambda b, hp: (b, hp)),
                  pl.BlockSpec((n_keys, LANES), lambda b, hp: (b, hp)),
                  pl.BlockSpec((n_keys, LANES), lambda b, hp: (b, hp)),
                  pl.BlockSpec((1, n_keys, LANES), lambda b, hp: (b, 0, 0)),
                  pl.BlockSpec((seq_q, LANES), lambda b, hp: (b, hp))],
        out_specs=pl.BlockSpec((seq_q, LANES), lambda b, hp: (b, hp)),
        out_shape=jax.ShapeDtypeStruct((batch * seq_q, D_MODEL), BF16),
        compiler_params=_cparams(("parallel", "parallel")),
        name="fox_attention",
    )(q, k_all, v_all, c_all, sg)


def _pad_cols(w, n):
    return jnp.pad(w, ((0, 0), (0, n - w.shape[1])))


def _prepare(norm_mix, norm_ffn, w_gla_in, w_gla_gate2, b_gla_gate, g_gla_head, w_gla_out, g_kv, w_kv,
             b_forget, g_k, w_fox_qg, g_q, w_fox_out, w_group, b_group, w_router, b_router):
    row = lambda a: a.reshape(1, -1).astype(F32)
    w_in = w_gla_in[0]
    c0, c1, c2, c3 = 2 * GLA_QK, 2 * GLA_QK + GLA_V, 2 * GLA_QK + GLA_V + GLA_GATE_RANK, w_in.shape[1]
    gla = dict(
        g=row(norm_mix[0]),
        w_qk=w_in[:, :c0].astype(BF16), w_v=w_in[:, c0:c1].astype(BF16), w_r=w_in[:, c2:c3].astype(BF16),
        w_g=_pad_cols(w_in[:, c1:c2], LANES).astype(BF16),
        w_g2=jnp.pad(w_gla_gate2[0], ((0, LANES - GLA_GATE_RANK), (0, 0))).astype(BF16),
        b_g=row(b_gla_gate[0]), g_head=row(g_gla_head[0]), w_out=w_gla_out[0].astype(BF16),
    )
    head_id = jnp.arange(D_MODEL) // FOX_DH
    fox = dict(
        g_kv=row(g_kv), g_mix=row(norm_mix[1]),
        w_k=w_kv[:, :D_MODEL].astype(BF16), w_v=w_kv[:, D_MODEL:2 * D_MODEL].astype(BF16),
        w_f=_pad_cols(w_kv[:, 2 * D_MODEL:], LANES).astype(BF16),
        b_f=_pad_cols(row(b_forget), LANES),
        w_q=w_fox_qg[0][:, :D_MODEL].astype(BF16), w_g=w_fox_qg[0][:, D_MODEL:].astype(BF16),
        g_k=row(jnp.tile(g_k, FOX_HEADS)), g_q=row(jnp.tile(g_q[0], FOX_HEADS)) * (FOX_DH ** -0.5 * LOG2E),
        head_sum=(head_id[:, None] == jnp.arange(LANES)[None, :]).astype(BF16),
        head_expand=(jnp.arange(LANES)[:, None] == head_id[None, :]).astype(BF16),
        w_out=w_fox_out[0].astype(BF16),
    )
    moe = []
    for layer in range(2):
        w_r = _pad_cols(jnp.concatenate([w_group[layer], w_router[layer]], axis=1), LANES).T
        w_r_hi = w_r.astype(BF16)
        moe.append(dict(
            g=row(norm_ffn[layer]),
            w_r_hi=w_r_hi, w_r_lo=(w_r - w_r_hi.astype(F32)).astype(BF16),
            b_r=_pad_cols(jnp.concatenate([row(b_group[layer]), row(b_router[layer])], axis=1), LANES).T,
        ))
    return gla, fox, moe


def _trunk(x, s0, past, gla, fox, moe, experts):
    batch, seq, _ = x.shape
    t = batch * seq
    h = x.reshape(t, D_MODEL)

    q, k, v, r, la = _gla_in_proj(h, gla["g"], gla["w_qk"], gla["w_v"], gla["w_r"], gla["w_g"], gla["w_g2"],
                                  gla["b_g"])
    o, s_new = _gla(q, k, v, la, s0, gla["g_head"], batch, seq)
    by_pair = _bins_are_pairs(t)
    routed = _out_proj(o, r, gla["w_out"], h, moe[0], by_pair)
    h, k_new, v_new, lf_new, qf, sg = _moe(routed, *experts, 0, fox)
    lf3 = lf_new.reshape(batch, seq, LANES)
    if past is None:
        n_past, n_keys = 0, seq
        tq = min(seq, 512)
        k_all, v_all, lf_all = k_new, v_new, lf3
    else:
        past_k, past_v, past_lf = past
        n_past = past_k.shape[1]
        tq = seq
        n_keys = -(-(n_past + seq) // KEY_ALIGN) * KEY_ALIGN
        pad = n_keys - n_past - seq
        cat = lambda a, b: jnp.concatenate(
            [a, b, jnp.zeros((batch, pad, a.shape[2]), F32)], axis=1)
        k_all = cat(past_k.reshape(batch, n_past, D_MODEL), k_new.reshape(batch, seq, D_MODEL)
                    ).reshape(batch * n_keys, D_MODEL)
        v_all = cat(past_v.reshape(batch, n_past, D_MODEL), v_new.reshape(batch, seq, D_MODEL)
                    ).reshape(batch * n_keys, D_MODEL)
        lf_all = cat(jnp.pad(past_lf, ((0, 0), (0, 0), (0, LANES - FOX_HEADS))), lf3)
    c_all = _cumsum(lf_all)
    o = _fox_attention(qf, sg, k_all, v_all, c_all, batch, seq, n_keys, n_past, tq)
    routed = _out_proj(o, None, fox["w_out"], h, moe[1], by_pair)
    h = _moe(routed, *experts, 1)

    return (h.reshape(batch, seq, D_MODEL),
            k_new.reshape(batch, seq, FOX_HEADS, FOX_DH),
            v_new.reshape(batch, seq, FOX_HEADS, FOX_DH),
            lf3[:, :, :FOX_HEADS],
            s_new[None])


def kernel(x_prompt, x_sample, cache_k, cache_v, cache_logf, state_gla, norm_mix, norm_ffn, w_gla_in, w_gla_gate2, b_gla_gate, g_gla_head, w_gla_out, g_kv, w_kv, b_forget, g_k, w_fox_qg, g_q, w_fox_out, w_group, b_group, w_router, b_router, w_exp_gate, w_exp_up, w_exp_down):
    gla, fox, moe = _prepare(norm_mix, norm_ffn, w_gla_in, w_gla_gate2, b_gla_gate, g_gla_head, w_gla_out, g_kv,
                             w_kv, b_forget, g_k, w_fox_qg, g_q, w_fox_out, w_group, b_group, w_router, b_router)
    experts = (w_exp_gate.astype(F32), w_exp_up.astype(F32), w_exp_down.astype(F32))
    s_zero = jnp.zeros((x_prompt.shape[0], GLA_HEADS, GLA_DK, GLA_DV), F32)
    y_p, k_p, v_p, lf_p, s_p = _trunk(x_prompt, s_zero, None, gla, fox, moe, experts)
    y_s, k_s, v_s, lf_s, s_s = _trunk(x_sample, state_gla[0].astype(F32),
                                      (cache_k.astype(F32), cache_v.astype(F32), cache_logf.astype(F32)),
                                      gla, fox, moe, experts)
    return (y_p, y_s, k_p, v_p, lf_p, s_p, k_s, v_s, lf_s, s_s)
```
